```python
import math
import jax, jax.numpy as jnp
from jax import lax
import numpy as np

D_MODEL = 4096
BATCH = 8
SEQ = 2048
DEPTH = 2
DEC_BATCH = 32
DEC_SEQ = 32
PAST_LEN = 1024

CHUNK = 64
Q_BLOCK = 128
N_BRANCH = 3
BRANCH_W = 2048
A_HEADS = 8
A_HD = 128
A_VD = 2 * A_HD
M_INNER = 2048
M_HEAD_DIM = 64
M_HEADS = M_INNER // M_HEAD_DIM
M_GROUPS = 8
M_STATE = 128
M_CONV = 4
M_CONV_CH = M_INNER + 2 * M_GROUPS * M_STATE
SSD_BLOCK = CHUNK
F_HEADS = 16
F_HD = 128
D_FF = 2 * D_MODEL
ROPE_THETA = 500000.0
ROPE_FRACTION = 4
EPS = 1e-6
NEG_INF = -1e30

SPLIT_SIZES = (A_HEADS * 2 * A_HD, A_HEADS * 2 * A_HD, A_HEADS * A_VD,
               M_INNER, M_CONV_CH, M_HEADS,
               F_HEADS * F_HD, F_HEADS * F_HD, F_HEADS * F_HD, F_HEADS,
               N_BRANCH * D_MODEL)
N_IN = sum(SPLIT_SIZES)
SPLIT_IDX = tuple(sum(SPLIT_SIZES[:i + 1]) for i in range(len(SPLIT_SIZES) - 1))

kernel_name = 'streaming_hybrid_diffattn_ssd_fox'


def _rms(x, g):
    xf = x.astype(jnp.float32)
    y = xf * lax.rsqrt(jnp.mean(jnp.square(xf), axis=-1, keepdims=True) + EPS)
    return (y * g.astype(jnp.float32)).astype(x.dtype)


def _swiglu(h, w_up, w_down):
    g, u = jnp.split(h @ w_up, 2, axis=-1)
    return (jax.nn.silu(g) * u) @ w_down


def _rope(x, pos):
    rot = x.shape[-1] // ROPE_FRACTION
    half = rot // 2
    inv_freq = jnp.float32(ROPE_THETA) ** (-jnp.arange(half, dtype=jnp.float32) / half)
    ang = pos.astype(jnp.float32)[:, None] * inv_freq[None, :]
    shape = (pos.shape[0],) + (1,) * (x.ndim - 3) + (half,)
    cos = jnp.cos(ang).reshape(shape)
    sin = jnp.sin(ang).reshape(shape)
    xr = x[..., :rot].astype(jnp.float32)
    x1, x2 = xr[..., :half], xr[..., half:]
    rotated = jnp.concatenate([x1 * cos - x2 * sin, x2 * cos + x1 * sin], axis=-1).astype(x.dtype)
    return jnp.concatenate([rotated, x[..., rot:]], axis=-1)


def _sweep(fn, q_side, q_pos):
    tq = q_pos.shape[0]
    blk = min(Q_BLOCK, tq)
    nb = tq // blk

    def split(a):
        return jnp.moveaxis(a.reshape((a.shape[0], nb, blk) + a.shape[2:]), 1, 0)

    out = lax.map(fn, (tuple(split(a) for a in q_side), q_pos.reshape(nb, blk)))
    out = jnp.moveaxis(out, 0, 1)
    return out.reshape((out.shape[0], tq) + out.shape[3:])


def _diff_attention(q, k, v, q_pos, k_pos, lam):
    scale = A_HD ** -0.5

    def block(args):
        (qb,), qp = args
        mask = (k_pos[None, :] // CHUNK) <= (qp[:, None] // CHUNK)
        s = jnp.einsum('bqhcd,bkhcd->bchqk', qb, k).astype(jnp.float32) * scale
        p = jax.nn.softmax(jnp.where(mask, s, NEG_INF), axis=-1)
        p = p[:, 0] - lam * p[:, 1]
        return jnp.einsum('bhqk,bkhe->bqhe', p.astype(v.dtype), v)

    return _sweep(block, (q,), q_pos)


def _forgetting_attention(q, k, v, cq, ck, q_pos, k_pos):
    scale = F_HD ** -0.5
    ck_t = jnp.moveaxis(ck, 1, 2)

    def block(args):
        (qb, cqb), qp = args
        mask = k_pos[None, :] <= qp[:, None]
        s = jnp.einsum('bqhd,bkhd->bhqk', qb, k).astype(jnp.float32) * scale
        s = s + jnp.moveaxis(cqb, 1, 2)[..., :, None] - ck_t[:, :, None, :]
        p = jax.nn.softmax(jnp.where(mask, s, NEG_INF), axis=-1)
        return jnp.einsum('bhqk,bkhd->bqhd', p.astype(v.dtype), v)

    return _sweep(block, (q, cq), q_pos)


def _ssd(x, dt, a, bm, cm, h0):
    bsz, t = x.shape[:2]
    L = min(SSD_BLOCK, t)
    nc = t // L
    f32 = jnp.float32
    xr = x.reshape((bsz, nc, L) + x.shape[2:]).astype(f32)
    dtr = dt.reshape((bsz, nc, L) + dt.shape[2:]).astype(f32)
    br = bm.reshape((bsz, nc, L) + bm.shape[2:]).astype(f32)
    cr = cm.reshape((bsz, nc, L) + cm.shape[2:]).astype(f32)
    cum = jnp.cumsum(dtr * a, axis=2)
    causal = jnp.tril(jnp.ones((L, L), dtype=bool))[:, :, None, None]
    seg = cum[:, :, :, None] - cum[:, :, None, :]
    decay = jnp.exp(jnp.where(causal, seg, -jnp.inf))
    xdt = xr * dtr[..., None]
    cb = jnp.einsum('bctgn,bcsgn->bctsg', cr, br)
    y_diag = jnp.einsum('bctsg,bctsge,bcsgep->bctgep', cb, decay, xdt)
    to_end = jnp.exp(cum[:, :, -1:] - cum)
    states = jnp.einsum('bcsgn,bcsge,bcsgep->bcgepn', br, to_end, xdt)
    block_decay = jnp.exp(cum[:, :, -1])

    def step(h, inp):
        s, d = inp
        return h * d[..., None, None] + s, h

    h_last, h_in = lax.scan(step, h0, (jnp.moveaxis(states, 1, 0), jnp.moveaxis(block_decay, 1, 0)))
    h_in = jnp.moveaxis(h_in, 0, 1)
    y_off = jnp.einsum('bctgn,bctge,bcgepn->bctgep', cr, jnp.exp(cum), h_in)
    y = (y_diag + y_off).reshape(x.shape)
    return y, h_last


def _mamba2(z, xbc, dt_raw, conv0, ssm0, conv_w, conv_b, dt_bias, a_log, d_skip, norm_g):
    bsz, t = z.shape[:2]
    f32 = jnp.float32
    e = M_HEADS // M_GROUPS
    xpad = jnp.concatenate([conv0.astype(xbc.dtype), xbc], axis=1)
    conv1 = xpad[:, xpad.shape[1] - (M_CONV - 1):]
    y = lax.conv_general_dilated(xpad, conv_w[:, None, :].astype(xpad.dtype), window_strides=(1,),
                                 padding='VALID', dimension_numbers=('NWC', 'WIO', 'NWC'),
                                 feature_group_count=M_CONV_CH)
    xbc = jax.nn.silu(y + conv_b)
    xs, bm, cm = jnp.split(xbc, [M_INNER, M_INNER + M_GROUPS * M_STATE], axis=-1)
    xs = xs.reshape(bsz, t, M_GROUPS, e, M_HEAD_DIM)
    bm = bm.reshape(bsz, t, M_GROUPS, M_STATE)
    cm = cm.reshape(bsz, t, M_GROUPS, M_STATE)
    dt = jax.nn.softplus((dt_raw + dt_bias).astype(f32)).reshape(bsz, t, M_GROUPS, e)
    a = -jnp.exp(a_log.astype(f32)).reshape(M_GROUPS, e)
    h0 = ssm0.astype(f32).reshape(bsz, M_GROUPS, e, M_HEAD_DIM, M_STATE)
    ys, h1 = _ssd(xs, dt, a, bm, cm, h0)
    ys = ys + d_skip.astype(f32).reshape(M_GROUPS, e)[..., None] * xs.astype(f32)
    ys = ys.reshape(bsz, t, M_INNER) * jax.nn.silu(z.astype(f32))
    ys = _rms(ys.reshape(bsz, t, M_GROUPS, M_INNER // M_GROUPS),
              norm_g.reshape(M_GROUPS, M_INNER // M_GROUPS)).reshape(bsz, t, M_INNER)
    return ys.astype(z.dtype), conv1, h1.reshape(bsz, M_HEADS, M_HEAD_DIM, M_STATE).astype(z.dtype)


def _empty_state(bsz, dtype):
    return (jnp.zeros((bsz, 0, A_HEADS, 2, A_HD), dtype),
            jnp.zeros((bsz, 0, A_HEADS, A_VD), dtype),
            jnp.zeros((bsz, M_CONV - 1, M_CONV_CH), dtype),
            jnp.zeros((bsz, M_HEADS, M_HEAD_DIM, M_STATE), dtype),
            jnp.zeros((bsz, 0, F_HEADS, F_HD), dtype),
            jnp.zeros((bsz, 0, F_HEADS, F_HD), dtype),
            jnp.zeros((bsz, 0, F_HEADS), dtype))


def _layer(x, past_len, st, p, l):
    a_k0, a_v0, conv0, ssm0, f_k0, f_v0, f_lf0 = st
    bsz, t = x.shape[:2]
    q_pos = past_len + jnp.arange(t, dtype=jnp.int32)
    k_pos = jnp.arange(past_len + t, dtype=jnp.int32)

    x = x + 0.5 * _swiglu(_rms(x, p['norm_ffn1']), p['w_ffn1_in'], p['w_ffn1_out'])

    h = _rms(x, p['norm_mix'])
    aq, ak, av, mz, mxbc, mdt, fq, fk, fv, ff, gt = jnp.split(h @ p['w_in'], SPLIT_IDX, axis=-1)

    lam_init = 0.8 - 0.6 * math.exp(-0.3 * l)
    aq = _rope(_rms(aq.reshape(bsz, t, A_HEADS, 2, A_HD), p['a_q_norm']), q_pos)
    ak = _rope(_rms(ak.reshape(bsz, t, A_HEADS, 2, A_HD), p['a_k_norm']), q_pos)
    av = av.reshape(bsz, t, A_HEADS, A_VD)
    lv = p['a_lambda'].astype(jnp.float32)
    lam = jnp.exp(jnp.sum(lv[0] * lv[1])) - jnp.exp(jnp.sum(lv[2] * lv[3])) + lam_init
    oa = _diff_attention(aq, jnp.concatenate([a_k0.astype(ak.dtype), ak], axis=1),
                         jnp.concatenate([a_v0.astype(av.dtype), av], axis=1), q_pos, k_pos, lam)
    oa = (_rms(oa, p['a_out_norm']) * (1.0 - lam_init)).reshape(bsz, t, BRANCH_W)

    ob, conv1, ssm1 = _mamba2(mz, mxbc, mdt, conv0, ssm0, p['m_conv_w'], p['m_conv_b'],
                              p['m_dt_bias'], p['m_a_log'], p['m_d'], p['m_norm'])

    fq = _rms(fq.reshape(bsz, t, F_HEADS, F_HD), p['f_q_norm'])
    fk = _rms(fk.reshape(bsz, t, F_HEADS, F_HD), p['f_k_norm'])
    fv = fv.reshape(bsz, t, F_HEADS, F_HD)
    lf = jax.nn.log_sigmoid((ff + p['f_b']).astype(jnp.float32))
    c_all = jnp.cumsum(jnp.concatenate([f_lf0.astype(jnp.float32), lf], axis=1), axis=1)
    oc = _forgetting_attention(fq, jnp.concatenate([f_k0.astype(fk.dtype), fk], axis=1),
                               jnp.concatenate([f_v0.astype(fv.dtype), fv], axis=1),
                               c_all[:, past_len:], c_all, q_pos, k_pos)
    oc = oc.reshape(bsz, t, BRANCH_W)

    g = jax.nn.sigmoid((gt.reshape(bsz, t, N_BRANCH, D_MODEL) + p['b_gate']).astype(jnp.float32)).astype(x.dtype)
    wb = p['w_branch']
    merged = g[:, :, 0] * (oa @ wb[0]) + g[:, :, 1] * (ob @ wb[1]) + g[:, :, 2] * (oc @ wb[2])
    x = x + merged @ p['w_out']

    x = x + 0.5 * _swiglu(_rms(x, p['norm_ffn2']), p['w_ffn2_in'], p['w_ffn2_out'])
    return x, (ak, av, conv1, ssm1, fk, fv, lf.astype(x.dtype))


def setup_inputs(seed: int = 0) -> dict:
    key = jax.random.key(seed)
    keys = iter(jax.random.split(key, 48))
    f32 = jnp.float32

    def nrm(shape, scale=1.0):
        return scale * jax.random.normal(next(keys), shape, f32)

    def gain(shape):
        return 1.0 + 0.05 * nrm(shape)

    dt0 = jnp.exp(jax.random.uniform(next(keys), (DEPTH, M_HEADS), f32, math.log(1e-3), math.log(1e-1)))
    a0 = jax.random.uniform(next(keys), (DEPTH, M_HEADS), f32, 1.0, 16.0)
    return {
        'x_prompt': nrm((BATCH, SEQ, D_MODEL)),
        'x_sample': nrm((DEC_BATCH, DEC_SEQ, D_MODEL)),
        'cache_a_k': nrm((DEPTH, DEC_BATCH, PAST_LEN, A_HEADS, 2, A_HD)),
        'cache_a_v': nrm((DEPTH, DEC_BATCH, PAST_LEN, A_HEADS, A_VD)),
        'state_conv': nrm((DEPTH, DEC_BATCH, M_CONV - 1, M_CONV_CH)),
        'state_ssm': nrm((DEPTH, DEC_BATCH, M_HEADS, M_HEAD_DIM, M_STATE), 0.5),
        'cache_f_k': nrm((DEPTH, DEC_BATCH, PAST_LEN, F_HEADS, F_HD)),
        'cache_f_v': nrm((DEPTH, DEC_BATCH, PAST_LEN, F_HEADS, F_HD)),
        'cache_f_logf': jax.nn.log_sigmoid(2.0 + nrm((DEPTH, DEC_BATCH, PAST_LEN, F_HEADS), 0.5)),
        'norm_ffn1': gain((DEPTH, D_MODEL)),
        'w_ffn1_in': nrm((DEPTH, D_MODEL, 2 * D_FF), D_MODEL ** -0.5),
        'w_ffn1_out': nrm((DEPTH, D_FF, D_MODEL), D_FF ** -0.5),
        'norm_mix': gain((DEPTH, D_MODEL)),
        'w_in': nrm((DEPTH, D_MODEL, N_IN), D_MODEL ** -0.5),
        'b_gate': nrm((DEPTH, N_BRANCH, D_MODEL), 0.02),
        'a_q_norm': gain((DEPTH, A_HD)),
        'a_k_norm': gain((DEPTH, A_HD)),
        'a_lambda': nrm((DEPTH, 4, A_HD), 0.1),
        'a_out_norm': gain((DEPTH, A_VD)),
        'm_conv_w': nrm((DEPTH, M_CONV, M_CONV_CH), M_CONV ** -0.5),
        'm_conv_b': nrm((DEPTH, M_CONV_CH), 0.02),
        'm_dt_bias': dt0 + jnp.log(-jnp.expm1(-dt0)),
        'm_a_log': jnp.log(a0),
        'm_d': 1.0 + 0.1 * nrm((DEPTH, M_HEADS)),
        'm_norm': gain((DEPTH, M_INNER)),
        'f_q_norm': gain((DEPTH, F_HD)),
        'f_k_norm': gain((DEPTH, F_HD)),
        'f_b': 2.0 + nrm((DEPTH, F_HEADS), 0.5),
        'w_branch': nrm((DEPTH, N_BRANCH, BRANCH_W, D_MODEL), BRANCH_W ** -0.5),
        'w_out': nrm((DEPTH, D_MODEL, D_MODEL), D_MODEL ** -0.5),
        'norm_ffn2': gain((DEPTH, D_MODEL)),
        'w_ffn2_in': nrm((DEPTH, D_MODEL, 2 * D_FF), D_MODEL ** -0.5),
        'w_ffn2_out': nrm((DEPTH, D_FF, D_MODEL), D_FF ** -0.5),
    }


def reference(x_prompt, x_sample, cache_a_k, cache_a_v, state_conv, state_ssm, cache_f_k, cache_f_v,
              cache_f_logf, norm_ffn1, w_ffn1_in, w_ffn1_out, norm_mix, w_in, b_gate, a_q_norm, a_k_norm,
              a_lambda, a_out_norm, m_conv_w, m_conv_b, m_dt_bias, m_a_log, m_d, m_norm, f_q_norm,
              f_k_norm, f_b, w_branch, w_out, norm_ffn2, w_ffn2_in, w_ffn2_out):
    past_len = cache_a_k.shape[2]
    y_p, y_s = x_prompt, x_sample
    new_p, new_s = [], []
    for l in range(DEPTH):
        p = dict(norm_ffn1=norm_ffn1[l], w_ffn1_in=w_ffn1_in[l], w_ffn1_out=w_ffn1_out[l],
                 norm_mix=norm_mix[l], w_in=w_in[l], b_gate=b_gate[l],
                 a_q_norm=a_q_norm[l], a_k_norm=a_k_norm[l], a_lambda=a_lambda[l], a_out_norm=a_out_norm[l],
                 m_conv_w=m_conv_w[l], m_conv_b=m_conv_b[l], m_dt_bias=m_dt_bias[l], m_a_log=m_a_log[l],
                 m_d=m_d[l], m_norm=m_norm[l], f_q_norm=f_q_norm[l], f_k_norm=f_k_norm[l], f_b=f_b[l],
                 w_branch=w_branch[l], w_out=w_out[l],
                 norm_ffn2=norm_ffn2[l], w_ffn2_in=w_ffn2_in[l], w_ffn2_out=w_ffn2_out[l])
        st_s = (cache_a_k[l], cache_a_v[l], state_conv[l], state_ssm[l], cache_f_k[l], cache_f_v[l], cache_f_logf[l])
        y_p, n_p = _layer(y_p, 0, _empty_state(y_p.shape[0], y_p.dtype), p, l)
        y_s, n_s = _layer(y_s, past_len, st_s, p, l)
        new_p.append(n_p)
        new_s.append(n_s)

    def stack(rows, i):
        return jnp.stack([r[i] for r in rows], axis=0)

    return (y_p, y_s,
            stack(new_p, 0), stack(new_p, 1), stack(new_p, 2), stack(new_p, 3),
            stack(new_p, 4), stack(new_p, 5), stack(new_p, 6),
            stack(new_s, 0), stack(new_s, 1), stack(new_s, 2), stack(new_s, 3),
            stack(new_s, 4), stack(new_s, 5), stack(new_s, 6))
```

```python
import functools
import math

import jax
import jax.numpy as jnp
from jax import lax
from jax.experimental import pallas as pl
from jax.experimental.pallas import tpu as pltpu

F32 = jnp.float32
BF16 = jnp.bfloat16

CHUNK = 64
N_BRANCH = 3
BRANCH_W = 2048
A_HEADS = 8
A_HD = 128
A_VD = 2 * A_HD
M_INNER = 2048
M_HEAD_DIM = 64
M_HEADS = M_INNER // M_HEAD_DIM
M_GROUPS = 8
M_GROUP_HEADS = M_HEADS // M_GROUPS
M_GROUP_W = M_INNER // M_GROUPS
M_STATE = 128
M_CONV = 4
M_CONV_CH = M_INNER + 2 * M_GROUPS * M_STATE
F_HEADS = 16
F_HD = 128
ROPE_THETA = 500000.0
ROPE_FRACTION = 4
EPS = 1e-6
NEG_INF = -1e30

LANES = 128
SUBLANES = 8
V7X_VMEM_BYTES = 64 * 1024 * 1024
VMEM_CAP = V7X_VMEM_BYTES - 6 * 1024 * 1024
VMEM_INTERNAL = 4 * 1024 * 1024

ATTN_BLOCK = 256
SSD_CHUNK = 256
SMALL_W = LANES
DT_LANE0 = 0
FF_LANE0 = M_HEADS


def _params(sem, est_bytes):
    return pltpu.CompilerParams(dimension_semantics=sem,
                                vmem_limit_bytes=int(min(VMEM_CAP, est_bytes + VMEM_INTERNAL)))


def _nbytes(shape, dtype):
    return math.prod(shape) * jnp.dtype(dtype).itemsize


def _tile(n, pref, align=0):
    t = pref
    while t > SUBLANES and (n % t or align % t):
        t //= 2
    assert n % t == 0 and align % t == 0
    return t


def _sigmoid(x):
    return 1.0 / (1.0 + jnp.exp(-x))


def _silu(x):
    return x * _sigmoid(x)


def _softplus(x):
    return jnp.maximum(x, 0.0) + jnp.log(1.0 + jnp.exp(-jnp.abs(x)))


def _rms_rows(x, gain):
    return x * lax.rsqrt(jnp.mean(x * x, axis=-1, keepdims=True) + EPS) * gain


def _rope(y, tc, ta, tb):
    return y * tc + pltpu.roll(y, LANES - 16, 1) * ta + pltpu.roll(y, 16, 1) * tb


def _prefix_sum(x, axis):
    n = x.shape[axis]
    idx = lax.broadcasted_iota(jnp.int32, x.shape, axis)
    d = 1
    while d < n:
        x = x + jnp.where(idx >= d, pltpu.roll(x, d, axis), 0.0)
        d *= 2
    return x


def _rmsnorm_kernel(x_ref, g_ref, o_ref):
    o_ref[...] = _rms_rows(x_ref[...], g_ref[...]).astype(o_ref.dtype)


def _rmsnorm(x, gain3, layer):
    m, d = x.shape
    tm = _tile(m, 256)
    est = 2 * (_nbytes((tm, d), F32) + _nbytes((tm, d), BF16)) + 2 * _nbytes((tm, d), F32)
    return pl.pallas_call(
        _rmsnorm_kernel,
        grid=(m // tm,),
        in_specs=[pl.BlockSpec((tm, d), lambda i: (i, 0)),
                  pl.BlockSpec((None, 1, d), lambda i: (layer, 0, 0))],
        out_specs=pl.BlockSpec((tm, d), lambda i: (i, 0)),
        out_shape=jax.ShapeDtypeStruct((m, d), BF16),
        compiler_params=_params(("parallel",), est),
        name="rmsnorm",
    )(x, gain3)


def _dense_kernel(*refs, n_w, n_ex, n_out, nk, epilogue):
    a_ref = refs[0]
    w_refs = refs[1:1 + n_w]
    ex_refs = refs[1 + n_w:1 + n_w + n_ex]
    out_refs = refs[1 + n_w + n_ex:1 + n_w + n_ex + n_out]
    acc_refs = refs[1 + n_w + n_ex + n_out:]
    a = a_ref[...]
    parts = [jnp.dot(a, w[...], preferred_element_type=F32) for w in w_refs]
    if nk == 1:
        epilogue(parts, ex_refs, out_refs)
        return
    k = pl.program_id(2)

    @pl.when(k == 0)
    def _():
        for acc, p in zip(acc_refs, parts):
            acc[...] = p

    @pl.when((k > 0) & (k < nk - 1))
    def _():
        for acc, p in zip(acc_refs, parts):
            acc[...] += p

    @pl.when(k == nk - 1)
    def _():
        epilogue([acc[...] + p for acc, p in zip(acc_refs, parts)], ex_refs, out_refs)


def _dense(name, a, w, layer, col_starts, n_cols, tm, tn, epilogue, out_dtypes, extras=(), nk=1):
    m, kdim = a.shape
    assert w.shape[1] == kdim and kdim % nk == 0 and m % tm == 0 and n_cols % tn == 0
    tk = kdim // nk
    in_specs = [pl.BlockSpec((tm, tk), lambda i, j, k: (i, k))]
    for s in col_starts:
        assert s % tn == 0
        in_specs.append(pl.BlockSpec((None, tk, tn), lambda i, j, k, s=s: (layer, k, s // tn + j)))
    in_specs += [spec for _, spec in extras]
    out_specs = [pl.BlockSpec((tm, tn), lambda i, j, k: (i, j)) for _ in out_dtypes]
    out_shape = [jax.ShapeDtypeStruct((m, n_cols), dt) for dt in out_dtypes]
    scratch = [pltpu.VMEM((tm, tn), F32) for _ in col_starts] if nk > 1 else []
    est = 2 * (_nbytes((tm, tk), BF16) + len(col_starts) * _nbytes((tk, tn), BF16))
    est += 2 * sum(_nbytes((tm, tn), dt) for dt in out_dtypes)
    est += 2 * sum(_nbytes([d for d in spec.block_shape if d is not None], arr.dtype) for arr, spec in extras)
    est += (len(col_starts) * (2 if nk > 1 else 1) + 1) * _nbytes((tm, tn), F32)
    kern = functools.partial(_dense_kernel, n_w=len(col_starts), n_ex=len(extras), n_out=len(out_dtypes),
                             nk=nk, epilogue=epilogue)
    return pl.pallas_call(
        kern,
        grid=(m // tm, n_cols // tn, nk),
        in_specs=in_specs,
        out_specs=out_specs,
        out_shape=out_shape,
        scratch_shapes=scratch,
        compiler_params=_params(("parallel", "parallel", "arbitrary"), est),
        name=name,
    )(a, *([w] * len(col_starts)), *[arr for arr, _ in extras])


def _epi_cast(parts, ex, outs):
    outs[0][...] = parts[0].astype(outs[0].dtype)


def _epi_swiglu(parts, ex, outs):
    gate, up = parts
    outs[0][...] = (_silu(gate) * up).astype(outs[0].dtype)


def _epi_residual_half(parts, ex, outs):
    outs[0][...] = ex[0][...] + 0.5 * parts[0]


def _epi_residual(parts, ex, outs):
    outs[0][...] = ex[0][...] + parts[0]


def _epi_gate(parts, ex, outs):
    outs[0][...] = _sigmoid(parts[0] + ex[0][...]).astype(outs[0].dtype)


def _epi_norm(parts, ex, outs):
    gain = ex[0][...]
    acc = parts[0]
    for c in range(acc.shape[1] // LANES):
        sl = slice(c * LANES, (c + 1) * LANES)
        outs[0][:, sl] = _rms_rows(acc[:, sl], gain)


def _epi_norm_rope(parts, ex, outs):
    gain, tc, ta, tb = ex[0][...], ex[1][...], ex[2][...], ex[3][...]
    acc = parts[0]
    for c in range(acc.shape[1] // LANES):
        sl = slice(c * LANES, (c + 1) * LANES)
        outs[0][:, sl] = _rope(_rms_rows(acc[:, sl], gain), tc, ta, tb)


def _merge_kernel(oa_ref, ob_ref, oc_ref, wa_ref, wb_ref, wc_ref, ga_ref, gb_ref, gc_ref, o_ref):
    acc = ga_ref[...].astype(F32) * jnp.dot(oa_ref[...], wa_ref[...], preferred_element_type=F32)
    acc += gb_ref[...].astype(F32) * jnp.dot(ob_ref[...], wb_ref[...], preferred_element_type=F32)
    acc += gc_ref[...].astype(F32) * jnp.dot(oc_ref[...], wc_ref[...], preferred_element_type=F32)
    o_ref[...] = acc.astype(o_ref.dtype)


def _merge(oa, ob, oc, w_branch, gates, layer):
    m, bw = oa.shape
    d = w_branch.shape[-1]
    tm = _tile(m, 1024)
    tn = _tile(d, 512)
    nj = d // tn
    a_spec = pl.BlockSpec((tm, bw), lambda i, j: (i, 0))
    in_specs = [a_spec, a_spec, a_spec]
    in_specs += [pl.BlockSpec((None, None, bw, tn), lambda i, j, b=b: (layer, b, 0, j)) for b in range(N_BRANCH)]
    in_specs += [pl.BlockSpec((tm, tn), lambda i, j, b=b: (i, b * nj + j)) for b in range(N_BRANCH)]
    est = 2 * N_BRANCH * (_nbytes((tm, bw), BF16) + _nbytes((bw, tn), BF16) + _nbytes((tm, tn), BF16))
    est += 2 * _nbytes((tm, tn), BF16) + 3 * _nbytes((tm, tn), F32)
    return pl.pallas_call(
        _merge_kernel,
        grid=(m // tm, nj),
        in_specs=in_specs,
        out_specs=pl.BlockSpec((tm, tn), lambda i, j: (i, j)),
        out_shape=jax.ShapeDtypeStruct((m, d), BF16),
        compiler_params=_params(("parallel", "parallel"), est),
        name="merge",
    )(oa, ob, oc, w_branch, w_branch, w_branch, gates, gates, gates)


def _logf_cumsum_kernel(*refs, past):
    if past:
        raw_ref, fb_ref, hist_ref, lf_ref, c_ref = refs
    else:
        raw_ref, fb_ref, lf_ref, c_ref = refs
    x = raw_ref[...] + fb_ref[...]
    lf = jnp.minimum(x, 0.0) - jnp.log(1.0 + jnp.exp(-jnp.abs(x)))
    lf_ref[...] = lf
    if past:
        lf = jnp.concatenate([hist_ref[...], lf], axis=0)
    c_ref[...] = _prefix_sum(lf, 0)


def _logf_cumsum(raw, f_b3, layer, hist=None):
    b, t, w = raw.shape
    past = 0 if hist is None else hist.shape[1]
    in_specs = [pl.BlockSpec((None, t, w), lambda i: (i, 0, 0)),
                pl.BlockSpec((None, 1, w), lambda i: (layer, 0, 0))]
    args = [raw, f_b3]
    if past:
        in_specs.append(pl.BlockSpec((None, past, w), lambda i: (i, 0, 0)))
        args.append(hist)
    est = 12 * _nbytes((t + past, w), F32)
    return pl.pallas_call(
        functools.partial(_logf_cumsum_kernel, past=past),
        grid=(b,),
        in_specs=in_specs,
        out_specs=[pl.BlockSpec((None, t, w), lambda i: (i, 0, 0)),
                   pl.BlockSpec((None, past + t, w), lambda i: (i, 0, 0))],
        out_shape=[jax.ShapeDtypeStruct((b, t, w), F32), jax.ShapeDtypeStruct((b, past + t, w), F32)],
        compiler_params=_params(("parallel",), est),
        name="logf_cumsum",
    )(*args)


def _diff_lambda(lam_ref, lam_init):
    lv = lam_ref[...]
    return (jnp.exp(jnp.sum(lv[0:1] * lv[1:2], axis=-1, keepdims=True))
            - jnp.exp(jnp.sum(lv[2:3] * lv[3:4], axis=-1, keepdims=True)) + lam_init)


def _diff_q(q_ref, gq_ref, tc_ref, ta_ref, tb_ref):
    scale = A_HD ** -0.5
    out = []
    for c in range(2):
        x = q_ref[:, c * A_HD:(c + 1) * A_HD].astype(F32)
        y = _rope(_rms_rows(x, gq_ref[...]), tc_ref[...], ta_ref[...], tb_ref[...])
        out.append((y * scale).astype(BF16))
    return out


def _diff_finish(o0, o1, lam, go_ref, lam_init, o_ref):
    o = o0 - lam * o1
    o_ref[...] = (_rms_rows(o, go_ref[...]) * (1.0 - lam_init)).astype(o_ref.dtype)


def _diff_attn_prompt_kernel(q_ref, k_ref, v_ref, tc_ref, ta_ref, tb_ref, gq_ref, lam_ref, go_ref, o_ref,
                             k16, v16, q16, m_s, l_s, acc_s, *, bq, lam_init):
    qi = pl.program_id(2)

    @pl.when(qi == 0)
    def _():
        k16[...] = k_ref[...].astype(BF16)
        v16[...] = v_ref[...].astype(BF16)

    qs = _diff_q(q_ref, gq_ref, tc_ref, ta_ref, tb_ref)
    for c in range(2):
        q16[c] = qs[c]
    m_s[...] = jnp.full(m_s.shape, NEG_INF, F32)
    l_s[...] = jnp.zeros(l_s.shape, F32)
    acc_s[...] = jnp.zeros(acc_s.shape, F32)

    def step(kb, masked):
        start = pl.multiple_of(kb * bq, bq)
        ks = k16[pl.ds(start, bq), :]
        vs = v16[pl.ds(start, bq), :]
        for c in range(2):
            s = lax.dot_general(q16[c], ks[:, c * A_HD:(c + 1) * A_HD], (((1,), (1,)), ((), ())),
                                preferred_element_type=F32)
            if masked:
                r = lax.broadcasted_iota(jnp.int32, s.shape, 0)
                j = lax.broadcasted_iota(jnp.int32, s.shape, 1)
                s = jnp.where(j // CHUNK <= r // CHUNK, s, NEG_INF)
            m_old = m_s[c]
            m_new = jnp.maximum(m_old, jnp.max(s, axis=-1, keepdims=True))
            alpha = jnp.exp(m_old - m_new)
            p = jnp.exp(s - m_new)
            l_s[c] = alpha * l_s[c] + jnp.sum(p, axis=-1, keepdims=True)
            acc_s[c] = alpha * acc_s[c] + jnp.dot(p.astype(BF16), vs, preferred_element_type=F32)
            m_s[c] = m_new

    def body(kb, carry):
        step(kb, False)
        return carry

    lax.fori_loop(0, qi, body, 0)
    step(qi, True)
    _diff_finish(acc_s[0] / l_s[0], acc_s[1] / l_s[1], _diff_lambda(lam_ref, lam_init), go_ref, lam_init, o_ref)


def _diff_attn_prompt(q, k, v, tabs, gq3, lam3, go3, layer, lam_init):
    b, t, w = q.shape
    hw = 2 * A_HD
    bq = min(ATTN_BLOCK, t)
    nq = t // bq
    qspec = pl.BlockSpec((None, bq, hw), lambda bi, h, qi: (bi, qi, h))
    kvspec = pl.BlockSpec((None, t, hw), lambda bi, h, qi: (bi, 0, h))
    tspec = pl.BlockSpec((bq, LANES), lambda bi, h, qi: (qi, 0))
    in_specs = [qspec, kvspec, kvspec, tspec, tspec, tspec,
                pl.BlockSpec((None, 1, A_HD), lambda bi, h, qi: (layer, 0, 0)),
                pl.BlockSpec((None, 4, A_HD), lambda bi, h, qi: (layer, 0, 0)),
                pl.BlockSpec((None, 1, A_VD), lambda bi, h, qi: (layer, 0, 0))]
    scratch = [pltpu.VMEM((t, hw), BF16), pltpu.VMEM((t, hw), BF16), pltpu.VMEM((2, bq, A_HD), BF16),
               pltpu.VMEM((2, bq, 1), F32), pltpu.VMEM((2, bq, 1), F32), pltpu.VMEM((2, bq, hw), F32)]
    est = 4 * _nbytes((t, hw), F32) + 2 * _nbytes((t, hw), BF16) + 24 * _nbytes((bq, bq), F32)
    return pl.pallas_call(
        functools.partial(_diff_attn_prompt_kernel, bq=bq, lam_init=lam_init),
        grid=(b, A_HEADS, nq),
        in_specs=in_specs,
        out_specs=qspec,
        out_shape=jax.ShapeDtypeStruct((b, t, w), BF16),
        scratch_shapes=scratch,
        compiler_params=_params(("parallel", "parallel", "arbitrary"), est),
        name="diff_attn_prompt",
    )(q, k, v, *tabs, gq3, lam3, go3)


def _diff_attn_sample_kernel(q_ref, kc_ref, vc_ref, kn_ref, vn_ref, tc_ref, ta_ref, tb_ref, gq_ref, lam_ref,
                             go_ref, o_ref, *, past, lam_init):
    qs = _diff_q(q_ref, gq_ref, tc_ref, ta_ref, tb_ref)
    kc = kc_ref[...].astype(BF16)
    kn = kn_ref[...].astype(BF16)
    vc = vc_ref[...].astype(BF16)
    vn = vn_ref[...].astype(BF16)
    dn = (((1,), (1,)), ((), ()))
    outs = []
    for c in range(2):
        sl = slice(c * A_HD, (c + 1) * A_HD)
        sc = lax.dot_general(qs[c], kc[:, sl], dn, preferred_element_type=F32)
        sn = lax.dot_general(qs[c], kn[:, sl], dn, preferred_element_type=F32)
        qchunk = (past + lax.broadcasted_iota(jnp.int32, sc.shape, 0)) // CHUNK
        sc = jnp.where(lax.broadcasted_iota(jnp.int32, sc.shape, 1) // CHUNK <= qchunk, sc, NEG_INF)
        qchunk = (past + lax.broadcasted_iota(jnp.int32, sn.shape, 0)) // CHUNK
        sn = jnp.where((past + lax.broadcasted_iota(jnp.int32, sn.shape, 1)) // CHUNK <= qchunk, sn, NEG_INF)
        m = jnp.maximum(jnp.max(sc, axis=-1, keepdims=True), jnp.max(sn, axis=-1, keepdims=True))
        pc = jnp.exp(sc - m)
        pn = jnp.exp(sn - m)
        l = jnp.sum(pc, axis=-1, keepdims=True) + jnp.sum(pn, axis=-1, keepdims=True)
        o = (jnp.dot(pc.astype(BF16), vc, preferred_element_type=F32)
             + jnp.dot(pn.astype(BF16), vn, preferred_element_type=F32))
        outs.append(o / l)
    _diff_finish(outs[0], outs[1], _diff_lambda(lam_ref, lam_init), go_ref, lam_init, o_ref)


def _diff_attn_sample(q, kc4, vc4, kn, vn, tabs, gq3, lam3, go3, layer, lam_init):
    b, t, w = q.shape
    past = kc4.shape[2]
    hw = 2 * A_HD
    qspec = pl.BlockSpec((None, t, hw), lambda bi, h: (bi, 0, h))
    cspec = pl.BlockSpec((None, None, past, hw), lambda bi, h: (layer, bi, 0, h))
    tspec = pl.BlockSpec((t, LANES), lambda bi, h: (0, 0))
    in_specs = [qspec, cspec, cspec, qspec, qspec, tspec, tspec, tspec,
                pl.BlockSpec((None, 1, A_HD), lambda bi, h: (layer, 0, 0)),
                pl.BlockSpec((None, 4, A_HD), lambda bi, h: (layer, 0, 0)),
                pl.BlockSpec((None, 1, A_VD), lambda bi, h: (layer, 0, 0))]
    est = 6 * _nbytes((past + t, hw), F32) + 12 * _nbytes((t, past), F32)
    return pl.pallas_call(
        functools.partial(_diff_attn_sample_kernel, past=past, lam_init=lam_init),
        grid=(b, A_HEADS),
        in_specs=in_specs,
        out_specs=qspec,
        out_shape=jax.ShapeDtypeStruct((b, t, w), BF16),
        compiler_params=_params(("parallel", "parallel"), est),
        name="diff_attn_sample",
    )(q, kc4, vc4, kn, vn, *tabs, gq3, lam3, go3)


def _lane_pick(blk, lane_index):
    lane = lax.broadcasted_iota(jnp.int32, blk.shape, 1)
    return jnp.sum(jnp.where(lane == lane_index, blk, 0.0), axis=-1, keepdims=True)


def _fox_q(q_ref, gq_ref):
    return (_rms_rows(q_ref[...].astype(F32), gq_ref[...]) * (F_HD ** -0.5)).astype(BF16)


def _fox_attn_prompt_kernel(q_ref, k_ref, v_ref, cq_ref, ck_ref, gq_ref, o_ref,
                            k16, v16, m_s, l_s, acc_s, *, bq):
    h = pl.program_id(1)
    qi = pl.program_id(2)

    @pl.when(qi == 0)
    def _():
        k16[...] = k_ref[...].astype(BF16)
        v16[...] = v_ref[...].astype(BF16)

    q = _fox_q(q_ref, gq_ref)
    cq = _lane_pick(cq_ref[...], h)
    m_s[...] = jnp.full(m_s.shape, NEG_INF, F32)
    l_s[...] = jnp.zeros(l_s.shape, F32)
    acc_s[...] = jnp.zeros(acc_s.shape, F32)

    def step(kb, masked):
        start = pl.multiple_of(kb * bq, bq)
        s = lax.dot_general(q, k16[pl.ds(start, bq), :], (((1,), (1,)), ((), ())), preferred_element_type=F32)
        s = s + cq - ck_ref[kb]
        if masked:
            r = lax.broadcasted_iota(jnp.int32, s.shape, 0)
            j = lax.broadcasted_iota(jnp.int32, s.shape, 1)
            s = jnp.where(j <= r, s, NEG_INF)
        m_old = m_s[...]
        m_new = jnp.maximum(m_old, jnp.max(s, axis=-1, keepdims=True))
        alpha = jnp.exp(m_old - m_new)
        p = jnp.exp(s - m_new)
        l_s[...] = alpha * l_s[...] + jnp.sum(p, axis=-1, keepdims=True)
        acc_s[...] = alpha * acc_s[...] + jnp.dot(p.astype(BF16), v16[pl.ds(start, bq), :],
                                                  preferred_element_type=F32)
        m_s[...] = m_new

    def body(kb, carry):
        step(kb, False)
        return carry

    lax.fori_loop(0, qi, body, 0)
    step(qi, True)
    o_ref[...] = (acc_s[...] / l_s[...]).astype(o_ref.dtype)


def _fox_attn_prompt(q, k, v, c_col, c_row, gq3, layer):
    b, t, w = q.shape
    bq = min(ATTN_BLOCK, t)
    nq = t // bq
    qspec = pl.BlockSpec((None, bq, F_HD), lambda bi, h, qi: (bi, qi, h))
    kvspec = pl.BlockSpec((None, t, F_HD), lambda bi, h, qi: (bi, 0, h))
    in_specs = [qspec, kvspec, kvspec,
                pl.BlockSpec((None, bq, F_HEADS), lambda bi, h, qi: (bi, qi, 0)),
                pl.BlockSpec((None, None, nq, 1, bq), lambda bi, h, qi: (bi, h, 0, 0, 0)),
                pl.BlockSpec((None, 1, F_HD), lambda bi, h, qi: (layer, 0, 0))]
    scratch = [pltpu.VMEM((t, F_HD), BF16), pltpu.VMEM((t, F_HD), BF16),
               pltpu.VMEM((bq, 1), F32), pltpu.VMEM((bq, 1), F32), pltpu.VMEM((bq, F_HD), F32)]
    est = 4 * _nbytes((t, F_HD), F32) + 2 * _nbytes((t, F_HD), BF16) + 16 * _nbytes((bq, bq), F32)
    return pl.pallas_call(
        functools.partial(_fox_attn_prompt_kernel, bq=bq),
        grid=(b, F_HEADS, nq),
        in_specs=in_specs,
        out_specs=qspec,
        out_shape=jax.ShapeDtypeStruct((b, t, w), BF16),
        scratch_shapes=scratch,
        compiler_params=_params(("parallel", "parallel", "arbitrary"), est),
        name="fox_attn_prompt",
    )(q, k, v, c_col, c_row, gq3)


def _fox_attn_sample_kernel(q_ref, kc_ref, vc_ref, kn_ref, vn_ref, cq_ref, ck_ref, gq_ref, o_ref, *, past):
    h = pl.program_id(1)
    q = _fox_q(q_ref, gq_ref)
    cq = _lane_pick(cq_ref[...], h)
    dn = (((1,), (1,)), ((), ()))
    sc = lax.dot_general(q, kc_ref[...].astype(BF16), dn, preferred_element_type=F32) + cq - ck_ref[:, :past]
    sn = lax.dot_general(q, kn_ref[...].astype(BF16), dn, preferred_element_type=F32) + cq - ck_ref[:, past:]
    r = lax.broadcasted_iota(jnp.int32, sn.shape, 0)
    j = lax.broadcasted_iota(jnp.int32, sn.shape, 1)
    sn = jnp.where(j <= r, sn, NEG_INF)
    m = jnp.maximum(jnp.max(sc, axis=-1, keepdims=True), jnp.max(sn, axis=-1, keepdims=True))
    pc = jnp.exp(sc - m)
    pn = jnp.exp(sn - m)
    l = jnp.sum(pc, axis=-1, keepdims=True) + jnp.sum(pn, axis=-1, keepdims=True)
    o = (jnp.dot(pc.astype(BF16), vc_ref[...].astype(BF16), preferred_element_type=F32)
         + jnp.dot(pn.astype(BF16), vn_ref[...].astype(BF16), preferred_element_type=F32))
    o_ref[...] = (o / l).astype(o_ref.dtype)


def _fox_attn_sample(q, kc4, vc4, kn, vn, c_new, c_row, gq3, layer):
    b, t, w = q.shape
    past = kc4.shape[2]
    qspec = pl.BlockSpec((None, t, F_HD), lambda bi, h: (bi, 0, h))
    cspec = pl.BlockSpec((None, None, past, F_HD), lambda bi, h: (layer, bi, 0, h))
    in_specs = [qspec, cspec, cspec, qspec, qspec,
                pl.BlockSpec((None, t, F_HEADS), lambda bi, h: (bi, 0, 0)),
                pl.BlockSpec((None, None, 1, past + t), lambda bi, h: (bi, h, 0, 0)),
                pl.BlockSpec((None, 1, F_HD), lambda bi, h: (layer, 0, 0))]
    est = 6 * _nbytes((past + t, F_HD), F32) + 12 * _nbytes((t, past), F32)
    return pl.pallas_call(
        functools.partial(_fox_attn_sample_kernel, past=past),
        grid=(b, F_HEADS),
        in_specs=in_specs,
        out_specs=qspec,
        out_shape=jax.ShapeDtypeStruct((b, t, w), BF16),
        compiler_params=_params(("parallel", "parallel"), est),
        name="fox_attn_sample",
    )(q, kc4, vc4, kn, vn, c_new, c_row, gq3)


def _ssd_kernel(xs_ref, bm_ref, cm_ref, z_ref, dtc_ref, dtr_ref,
                cx_ref, cb_ref, cc_ref, h0_ref,
                wx_ref, wb_ref, wc_ref, bx_ref, bb_ref, bc_ref,
                dtb_c_ref, dtb_r_ref, al_c_ref, al_r_ref, d_ref, ng_ref,
                o_ref, h1_ref,
                pad_s, xs_s, bm_s, cm_s, h_s, cumr_s, *, t, tp, chunk):
    e_heads = M_GROUP_HEADS
    hd = M_HEAD_DIM
    nc = tp // chunk
    head0 = pl.program_id(1) * e_heads + DT_LANE0

    def conv(raw_ref, hist_ref, w_ref, b_ref, dst):
        wdt = raw_ref.shape[1]
        pad_s[SUBLANES - (M_CONV - 1):SUBLANES, :wdt] = hist_ref[...].astype(F32)
        pad_s[SUBLANES:SUBLANES + t, :wdt] = raw_ref[...].astype(F32)
        rb = min(t, 256)
        for r0 in range(0, t, rb):
            acc = jnp.zeros((rb, wdt), F32) + b_ref[...]
            for k in range(M_CONV):
                lo = r0 + SUBLANES - (M_CONV - 1) + k
                acc = acc + w_ref[k:k + 1, :] * pad_s[lo:lo + rb, :wdt]
            dst[r0:r0 + rb, :] = _silu(acc)
        if tp > t:
            dst[t:tp, :] = jnp.zeros((tp - t, wdt), F32)

    conv(xs_ref, cx_ref, wx_ref, bx_ref, xs_s)
    conv(bm_ref, cb_ref, wb_ref, bb_ref, bm_s)
    conv(cm_ref, cc_ref, wc_ref, bc_ref, cm_s)

    h_s[...] = h0_ref[...]
    a_c = -jnp.exp(al_c_ref[...])
    a_r = -jnp.exp(al_r_ref[...])
    causal = (lax.broadcasted_iota(jnp.int32, (chunk, chunk), 1)
              <= lax.broadcasted_iota(jnp.int32, (chunk, chunk), 0))
    lane = lax.broadcasted_iota(jnp.int32, (chunk, M_GROUP_W), 1)

    def spread(cols):
        out = cols[e_heads - 1]
        for e in range(e_heads - 2, -1, -1):
            out = jnp.where(lane[:cols[e].shape[0]] < (e + 1) * hd, cols[e], out)
        return out

    def body(c, carry):
        r0 = c * chunk if isinstance(c, int) else pl.multiple_of(c * chunk, chunk)
        rows = pl.ds(r0, chunk)
        dt_c = _softplus(dtc_ref[rows, :] + dtb_c_ref[...])
        dt_r = _softplus(dtr_ref[c] + dtb_r_ref[...])
        if tp > t:
            dt_c = jnp.where(r0 + lax.broadcasted_iota(jnp.int32, dt_c.shape, 0) < t, dt_c, 0.0)
            dt_r = jnp.where(r0 + lax.broadcasted_iota(jnp.int32, dt_r.shape, 1) < t, dt_r, 0.0)
        cum_c = _prefix_sum(dt_c * a_c, 0)
        cumr_s[...] = _prefix_sum(dt_r * a_r, 1)
        cum_cols = [_lane_pick(cum_c, head0 + e) for e in range(e_heads)]
        cum_rows = [cumr_s[pl.ds(head0 + e, 1), :] for e in range(e_heads)]
        xs = xs_s[rows, :]
        bm = bm_s[rows, :].astype(BF16)
        cm = cm_s[rows, :].astype(BF16)
        cum_x = spread(cum_cols)
        xdt = xs * spread([_lane_pick(dt_c, head0 + e) for e in range(e_heads)])
        xdt16 = xdt.astype(BF16)
        g = lax.dot_general(cm, bm, (((1,), (1,)), ((), ())), preferred_element_type=F32)
        ys = []
        for e in range(e_heads):
            seg = cum_cols[e] - cum_rows[e]
            mat = g * jnp.exp(jnp.where(causal, seg, -jnp.inf))
            ys.append(jnp.dot(mat.astype(BF16), xdt16[:, e * hd:(e + 1) * hd], preferred_element_type=F32))
        y = jnp.concatenate(ys, axis=1)
        h = h_s[...]
        y = y + jnp.exp(cum_x) * lax.dot_general(cm, h.astype(BF16), (((1,), (1,)), ((), ())),
                                                 preferred_element_type=F32)
        to_end = jnp.exp(cum_x[chunk - 1:chunk, :] - cum_x)
        st = lax.dot_general((xdt * to_end).astype(BF16), bm, (((0,), (0,)), ((), ())),
                             preferred_element_type=F32)
        for e in range(e_heads):
            dec = jnp.exp(cum_rows[e][:, chunk - 1:chunk])
            h_s[e * hd:(e + 1) * hd, :] = h[e * hd:(e + 1) * hd, :] * dec + st[e * hd:(e + 1) * hd, :]
        d_x = spread([_lane_pick(d_ref[...], head0 + e) for e in range(e_heads)])
        y = y + d_x * xs
        if tp > t:
            y = y[:t] * _silu(z_ref[...].astype(F32))
            o_ref[...] = _rms_rows(y, ng_ref[...]).astype(o_ref.dtype)
        else:
            y = y * _silu(z_ref[rows, :].astype(F32))
            o_ref[rows, :] = _rms_rows(y, ng_ref[...]).astype(o_ref.dtype)
        return carry

    if nc == 1:
        body(0, 0)
    else:
        assert tp == t
        lax.fori_loop(0, nc, body, 0)
    h1_ref[...] = h_s[...]


def _ssd(xbc, z, small, conv0, conv0_layer, ssm0, ssm0_layer, pr, layer):
    b, t, _ = xbc.shape
    chunk = min(SSD_CHUNK, -(-t // LANES) * LANES)
    tp = -(-t // chunk) * chunk
    nc = tp // chunk
    gw = M_GROUP_W
    ns = M_STATE
    b_off = M_INNER // ns
    c_off = (M_INNER + M_GROUPS * ns) // ns
    dt_col = jnp.pad(small, ((0, 0), (0, tp - t), (0, 0)))
    dt_row = dt_col.reshape(b, nc, chunk, SMALL_W).transpose(0, 1, 3, 2)

    def bspec(width, off):
        return pl.BlockSpec((None, t, width), lambda bi, g: (bi, 0, off + g))

    def hspec(width, off):
        return pl.BlockSpec((None, None, M_CONV - 1, width), lambda bi, g: (conv0_layer, bi, 0, off + g))

    def wspec(rows, width, off):
        return pl.BlockSpec((None, rows, width), lambda bi, g: (layer, 0, off + g))

    col_p = pl.BlockSpec((None, 1, SMALL_W), lambda bi, g: (layer, 0, 0))
    row_p = pl.BlockSpec((None, SMALL_W, 1), lambda bi, g: (layer, 0, 0))
    in_specs = [bspec(gw, 0), bspec(ns, b_off), bspec(ns, c_off), bspec(gw, 0),
                pl.BlockSpec((None, tp, SMALL_W), lambda bi, g: (bi, 0, 0)),
                pl.BlockSpec((None, nc, SMALL_W, chunk), lambda bi, g: (bi, 0, 0, 0)),
                hspec(gw, 0), hspec(ns, b_off), hspec(ns, c_off),
                pl.BlockSpec((None, None, gw, ns), lambda bi, g: (ssm0_layer, bi, g, 0)),
                wspec(M_CONV, gw, 0), wspec(M_CONV, ns, b_off), wspec(M_CONV, ns, c_off),
                wspec(1, gw, 0), wspec(1, ns, b_off), wspec(1, ns, c_off),
                col_p, row_p, col_p, row_p, col_p,
                wspec(1, gw, 0)]
    out_specs = [bspec(gw, 0), pl.BlockSpec((None, gw, ns), lambda bi, g: (bi, g, 0))]
    out_shape = [jax.ShapeDtypeStruct((b, t, M_INNER), BF16),
                 jax.ShapeDtypeStruct((b, M_HEADS * M_HEAD_DIM, ns), F32)]
    scratch = [pltpu.VMEM((t + SUBLANES, gw), F32), pltpu.VMEM((tp, gw), F32), pltpu.VMEM((tp, ns), F32),
               pltpu.VMEM((tp, ns), F32), pltpu.VMEM((gw, ns), F32), pltpu.VMEM((SMALL_W, chunk), F32)]
    est = 2 * (3 * _nbytes((t, gw), BF16) + 2 * _nbytes((t, ns), BF16) + 2 * _nbytes((tp, SMALL_W), F32))
    est += 2 * _nbytes((tp + SUBLANES, gw), F32) + 2 * _nbytes((tp, ns), F32)
    est += 32 * _nbytes((chunk, max(chunk, gw)), F32)
    return pl.pallas_call(
        functools.partial(_ssd_kernel, t=t, tp=tp, chunk=chunk),
        grid=(b, M_GROUPS),
        in_specs=in_specs,
        out_specs=out_specs,
        out_shape=out_shape,
        scratch_shapes=scratch,
        compiler_params=_params(("parallel", "parallel"), est),
        name="ssd",
    )(xbc, xbc, xbc, z, dt_col, dt_row, conv0, conv0, conv0, ssm0,
      pr["conv_w"], pr["conv_w"], pr["conv_w"], pr["conv_b"], pr["conv_b"], pr["conv_b"],
      pr["dt_bias_c"], pr["dt_bias_r"], pr["a_log_c"], pr["a_log_r"], pr["d_c"], pr["m_norm"])


def _rope_tables(pos):
    rot = A_HD // ROPE_FRACTION
    half = rot // 2
    inv_freq = jnp.float32(ROPE_THETA) ** (-jnp.arange(half, dtype=F32) / half)
    ang = pos.astype(F32)[:, None] * inv_freq[None, :]
    cos, sin = jnp.cos(ang), jnp.sin(ang)
    n = pos.shape[0]
    tc = jnp.concatenate([cos, cos, jnp.ones((n, A_HD - rot), F32)], axis=1)
    ta = jnp.concatenate([-sin, jnp.zeros((n, A_HD - half), F32)], axis=1)
    tb = jnp.concatenate([jnp.zeros((n, half), F32), sin, jnp.zeros((n, A_HD - rot), F32)], axis=1)
    return tc, ta, tb


def _lanes(p, lane0):
    depth, n = p.shape
    return jnp.pad(p, ((0, 0), (lane0, SMALL_W - lane0 - n))).reshape(depth, 1, SMALL_W)


def _layer(x, st, past, wts, pr, tabs_seq, layer, conv0_layer, ssm0_layer):
    b, t, d = x.shape
    m = b * t
    a_kc, a_vc, conv0, ssm0, f_kc, f_vc, f_lfc = st
    lam_init = 0.8 - 0.6 * math.exp(-0.3 * layer)
    d_ff = wts["ffn1_out"].shape[1]
    tm = _tile(m, 1024)
    x2 = x.reshape(m, d)

    def ffn(x2, norm_g, w_in, w_out):
        hn = _rmsnorm(x2, norm_g, layer)
        tn = _tile(d_ff, 512)
        (hid,) = _dense("ffn_up", hn, w_in, layer, (0, d_ff), d_ff, tm, tn, _epi_swiglu, (BF16,))
        tn = _tile(d, 512)
        res = (x2, pl.BlockSpec((tm, tn), lambda i, j, k: (i, j)))
        (y,) = _dense("ffn_down", hid, w_out, layer, (0,), d, tm, tn, _epi_residual_half, (F32,),
                      extras=(res,), nk=2)
        return y

    x2 = ffn(x2, pr["norm_ffn1"], wts["ffn1_in"], wts["ffn1_out"])

    hn = _rmsnorm(x2, pr["norm_mix"], layer)
    w_in = wts["w_in"]
    seg = wts["seg"]
    aw = A_HEADS * 2 * A_HD
    fw = F_HEADS * F_HD

    def proj(name, key, width, epi, dtype, extras=(), tn_pref=1024):
        tn = _tile(width, tn_pref, seg[key])
        (o,) = _dense(name, hn, w_in, layer, (seg[key],), width, tm, tn, epi, (dtype,), extras=extras)
        return o

    if tm <= t:
        nt = t // tm
        tab_specs = [(tb_, pl.BlockSpec((tm, LANES), lambda i, j, k: (i % nt, 0))) for tb_ in tabs_seq]
    else:
        tab_specs = [(jnp.tile(tb_, (tm // t, 1)), pl.BlockSpec((tm, LANES), lambda i, j, k: (0, 0)))
                     for tb_ in tabs_seq]

    def gain_spec(arr):
        return (arr, pl.BlockSpec((None, 1, arr.shape[-1]), lambda i, j, k: (layer, 0, 0)))

    aq = proj("proj_aq", "aq", aw, _epi_cast, BF16)
    ak = proj("proj_ak", "ak", aw, _epi_norm_rope, F32, extras=(gain_spec(pr["a_k_norm"]), *tab_specs))
    av = proj("proj_av", "av", aw, _epi_cast, F32)
    mz = proj("proj_mz", "mz", M_INNER, _epi_cast, BF16)
    mxbc = proj("proj_mxbc", "mxbc", M_CONV_CH, _epi_cast, BF16)
    fq = proj("proj_fq", "fq", fw, _epi_cast, BF16)
    fk = proj("proj_fk", "fk", fw, _epi_norm, F32, extras=(gain_spec(pr["f_k_norm"]),))
    fv = proj("proj_fv", "fv", fw, _epi_cast, F32)
    gtn = _tile(N_BRANCH * d, 1024, seg["gt"])
    bias = (pr["b_gate"], pl.BlockSpec((None, 1, gtn), lambda i, j, k: (layer, 0, j)))
    gates = proj("proj_gate", "gt", N_BRANCH * d, _epi_gate, BF16, extras=(bias,))
    small = proj("proj_small", "small", SMALL_W, _epi_cast, F32).reshape(b, t, SMALL_W)

    a_args = (tabs_seq, pr["a_q_norm"], pr["a_lambda"], pr["a_out_norm"], layer, lam_init)
    aq3, ak3, av3 = aq.reshape(b, t, aw), ak.reshape(b, t, aw), av.reshape(b, t, aw)
    if past:
        oa = _diff_attn_sample(aq3, a_kc, a_vc, ak3, av3, *a_args)
    else:
        oa = _diff_attn_prompt(aq3, ak3, av3, *a_args)

    xbc3 = mxbc.reshape(b, t, M_CONV_CH)
    ob, ssm1 = _ssd(xbc3, mz.reshape(b, t, M_INNER), small, conv0, conv0_layer, ssm0, ssm0_layer, pr, layer)
    conv1 = jnp.concatenate([conv0[conv0_layer], xbc3[:, t - min(t, M_CONV - 1):].astype(F32)],
                            axis=1)[:, -(M_CONV - 1):]

    ff_lanes = slice(FF_LANE0, FF_LANE0 + F_HEADS)
    hist = None
    if past:
        hist = jnp.pad(f_lfc[layer], ((0, 0), (0, 0), (FF_LANE0, SMALL_W - FF_LANE0 - F_HEADS)))
    lf_w, c_w = _logf_cumsum(small, pr["f_b"], layer, hist)
    lf = lf_w[:, :, ff_lanes]
    c_all = c_w[:, :, ff_lanes]
    c_new = c_all[:, past:]
    fq3, fk3, fv3 = fq.reshape(b, t, fw), fk.reshape(b, t, fw), fv.reshape(b, t, fw)
    if past:
        c_row = c_all.transpose(0, 2, 1).reshape(b, F_HEADS, 1, past + t)
        oc = _fox_attn_sample(fq3, f_kc, f_vc, fk3, fv3, c_new, c_row, pr["f_q_norm"], layer)
    else:
        bq = min(ATTN_BLOCK, t)
        c_row = c_all.transpose(0, 2, 1).reshape(b, F_HEADS, t // bq, 1, bq)
        oc = _fox_attn_prompt(fq3, fk3, fv3, c_new, c_row, pr["f_q_norm"], layer)

    merged = _merge(oa.reshape(m, BRANCH_W), ob.reshape(m, BRANCH_W), oc.reshape(m, BRANCH_W),
                    wts["w_branch"], gates, layer)
    tn = _tile(d, 512)
    res = (x2, pl.BlockSpec((tm, tn), lambda i, j, k: (i, j)))
    (x2,) = _dense("out_proj", merged, wts["w_out"], layer, (0,), d, tm, tn, _epi_residual, (F32,), extras=(res,))

    x2 = ffn(x2, pr["norm_ffn2"], wts["ffn2_in"], wts["ffn2_out"])

    new = (ak.reshape(b, t, A_HEADS, 2, A_HD), av.reshape(b, t, A_HEADS, A_VD), conv1,
           ssm1.reshape(b, M_HEADS, M_HEAD_DIM, M_STATE), fk.reshape(b, t, F_HEADS, F_HD),
           fv.reshape(b, t, F_HEADS, F_HD), lf)
    return x2.reshape(b, t, d), new


def _prep_w_in(w_in, d):
    aw = A_HEADS * 2 * A_HD
    fw = F_HEADS * F_HD
    sizes = (aw, aw, A_HEADS * A_VD, M_INNER, M_CONV_CH, M_HEADS, fw, fw, fw, F_HEADS, N_BRANCH * d)
    names = ("aq", "ak", "av", "mz", "mxbc", "mdt", "fq", "fk", "fv", "ff", "gt")
    assert sum(sizes) == w_in.shape[-1]
    start, src = 0, {}
    for n, s in zip(names, sizes):
        src[n] = (start, s)
        start += s
    order = ("aq", "ak", "av", "mz", "mxbc", "fq", "fk", "fv", "gt")
    seg, pieces, off = {}, [], 0
    for n in order:
        s0, s = src[n]
        seg[n] = off
        pieces.append(w_in[:, :, s0:s0 + s])
        off += s
    seg["small"] = off
    assert DT_LANE0 == 0 and FF_LANE0 == M_HEADS
    pieces.append(w_in[:, :, src["mdt"][0]:src["mdt"][0] + M_HEADS])
    pieces.append(w_in[:, :, src["ff"][0]:src["ff"][0] + F_HEADS])
    pieces.append(jnp.zeros(w_in.shape[:2] + (SMALL_W - M_HEADS - F_HEADS,), w_in.dtype))
    return jnp.concatenate(pieces, axis=-1).astype(BF16), seg


def kernel(x_prompt, x_sample, cache_a_k, cache_a_v, state_conv, state_ssm, cache_f_k, cache_f_v, cache_f_logf, norm_ffn1, w_ffn1_in, w_ffn1_out, norm_mix, w_in, b_gate, a_q_norm, a_k_norm, a_lambda, a_out_norm, m_conv_w, m_conv_b, m_dt_bias, m_a_log, m_d, m_norm, f_q_norm, f_k_norm, f_b, w_branch, w_out, norm_ffn2, w_ffn2_in, w_ffn2_out):
    depth = w_in.shape[0]
    bp, tp, d = x_prompt.shape
    bs, ts, _ = x_sample.shape
    past = cache_a_k.shape[2]

    w_in16, seg = _prep_w_in(w_in, d)
    wts = dict(ffn1_in=w_ffn1_in.astype(BF16), ffn1_out=w_ffn1_out.astype(BF16), w_in=w_in16, seg=seg,
               w_branch=w_branch.astype(BF16), w_out=w_out.astype(BF16),
               ffn2_in=w_ffn2_in.astype(BF16), ffn2_out=w_ffn2_out.astype(BF16))

    def row3(p):
        return p.reshape(depth, 1, p.shape[-1])

    dt_bias_w, a_log_w = _lanes(m_dt_bias, DT_LANE0), _lanes(m_a_log, DT_LANE0)
    pr = dict(norm_ffn1=row3(norm_ffn1), norm_mix=row3(norm_mix), norm_ffn2=row3(norm_ffn2),
              b_gate=b_gate.reshape(depth, 1, N_BRANCH * d),
              a_q_norm=row3(a_q_norm), a_k_norm=row3(a_k_norm), a_lambda=a_lambda, a_out_norm=row3(a_out_norm),
              conv_w=m_conv_w, conv_b=row3(m_conv_b), m_norm=row3(m_norm),
              dt_bias_c=dt_bias_w, dt_bias_r=dt_bias_w.reshape(depth, SMALL_W, 1),
              a_log_c=a_log_w, a_log_r=a_log_w.reshape(depth, SMALL_W, 1),
              d_c=_lanes(m_d, DT_LANE0),
              f_q_norm=row3(f_q_norm), f_k_norm=row3(f_k_norm), f_b=_lanes(f_b, FF_LANE0))

    tabs_p = _rope_tables(jnp.arange(tp, dtype=jnp.int32))
    tabs_s = _rope_tables(past + jnp.arange(ts, dtype=jnp.int32))

    st_p = (None, None, jnp.zeros((1, bp, M_CONV - 1, M_CONV_CH), F32),
            jnp.zeros((1, bp, M_HEADS * M_HEAD_DIM, M_STATE), F32), None, None, None)
    st_s = (cache_a_k.reshape(depth, bs, past, A_HEADS * 2 * A_HD), cache_a_v.reshape(depth, bs, past, A_HEADS * A_VD),
            state_conv, state_ssm.reshape(depth, bs, M_HEADS * M_HEAD_DIM, M_STATE),
            cache_f_k.reshape(depth, bs, past, F_HEADS * F_HD), cache_f_v.reshape(depth, bs, past, F_HEADS * F_HD),
            cache_f_logf)

    y_p, y_s = x_prompt, x_sample
    new_p, new_s = [], []
    for layer in range(depth):
        y_p, n_p = _layer(y_p, st_p, 0, wts, pr, tabs_p, layer, 0, 0)
        y_s, n_s = _layer(y_s, st_s, past, wts, pr, tabs_s, layer, layer, layer)
        new_p.append(n_p)
        new_s.append(n_s)

    def stack(rows, i):
        return jnp.stack([r[i] for r in rows], axis=0)

    return (y_p, y_s) + tuple(stack(new_p, i) for i in range(7)) + tuple(stack(new_s, i) for i in range(7))
```

```python
import functools
import math

import jax
import jax.numpy as jnp
from jax import lax
from jax.experimental import pallas as pl
from jax.experimental.pallas import tpu as pltpu

F32 = jnp.float32
BF16 = jnp.bfloat16

CHUNK = 64
N_BRANCH = 3
BRANCH_W = 2048
A_HEADS = 8
A_HD = 128
A_VD = 2 * A_HD
M_INNER = 2048
M_HEAD_DIM = 64
M_HEADS = M_INNER // M_HEAD_DIM
M_GROUPS = 8
M_GROUP_HEADS = M_HEADS // M_GROUPS
M_GROUP_W = M_INNER // M_GROUPS
M_STATE = 128
M_CONV = 4
M_CONV_CH = M_INNER + 2 * M_GROUPS * M_STATE
F_HEADS = 16
F_HD = 128
ROPE_THETA = 500000.0
ROPE_FRACTION = 4
EPS = 1e-6
NEG_INF = -1e30
LOG2E = 1.4426950408889634

LANES = 128
SUBLANES = 8
V7X_VMEM_BYTES = 64 * 1024 * 1024
VMEM_CAP = V7X_VMEM_BYTES - 6 * 1024 * 1024
VMEM_INTERNAL = 4 * 1024 * 1024

ATTN_BLOCK = 256
SSD_CHUNK = 256
SMALL_W = LANES
DT_LANE0 = 0
FF_LANE0 = M_HEADS
KV_ROWS = SUBLANES
KV_W = LANES


def _params(sem, est_bytes):
    return pltpu.CompilerParams(dimension_semantics=sem,
                                vmem_limit_bytes=int(min(VMEM_CAP, est_bytes + VMEM_INTERNAL)))


def _nbytes(shape, dtype):
    return math.prod(shape) * jnp.dtype(dtype).itemsize


def _tile(n, pref, align=0):
    t = pref
    while t > SUBLANES and (n % t or align % t):
        t //= 2
    assert n % t == 0 and align % t == 0
    return t


def _sigmoid(x):
    return 1.0 / (1.0 + jnp.exp(-x))


def _silu(x):
    return x * _sigmoid(x)


def _softplus(x):
    return jnp.maximum(x, 0.0) + jnp.log(1.0 + jnp.exp(-jnp.abs(x)))


def _rms_rows(x, gain):
    return x * lax.rsqrt(jnp.mean(x * x, axis=-1, keepdims=True) + EPS) * gain


def _rope(y, tc, ta, tb):
    return y * tc + pltpu.roll(y, LANES - 16, 1) * ta + pltpu.roll(y, 16, 1) * tb


def _prefix_sum(x, axis):
    n = x.shape[axis]
    idx = lax.broadcasted_iota(jnp.int32, x.shape, axis)
    d = 1
    while d < n:
        x = x + jnp.where(idx >= d, pltpu.roll(x, d, axis), 0.0)
        d *= 2
    return x


def _lane_pick(blk, lane_index):
    lane = lax.broadcasted_iota(jnp.int32, blk.shape, 1)
    return jnp.sum(jnp.where(lane == lane_index, blk, 0.0), axis=-1, keepdims=True)


def _token_major(x):
    tokens, rows, width = x.shape
    return x.reshape(tokens, rows * width).astype(BF16)


def _rmsnorm_kernel(x_ref, g_ref, o_ref):
    o_ref[...] = _rms_rows(x_ref[...], g_ref[...]).astype(o_ref.dtype)


def _rmsnorm(x, gain3, layer):
    m, d = x.shape
    tm = _tile(m, 256)
    est = 2 * (_nbytes((tm, d), F32) + _nbytes((tm, d), BF16)) + 2 * _nbytes((tm, d), F32)
    return pl.pallas_call(
        _rmsnorm_kernel,
        grid=(m // tm,),
        in_specs=[pl.BlockSpec((tm, d), lambda i: (i, 0)),
                  pl.BlockSpec((None, 1, d), lambda i: (layer, 0, 0))],
        out_specs=pl.BlockSpec((tm, d), lambda i: (i, 0)),
        out_shape=jax.ShapeDtypeStruct((m, d), BF16),
        compiler_params=_params(("parallel",), est),
        name="rmsnorm",
    )(x, gain3)


def _dense_kernel(*refs, n_w, n_ex, n_alias, n_out, nk, epilogue):
    a_ref = refs[0]
    w_refs = refs[1:1 + n_w]
    ex_refs = refs[1 + n_w:1 + n_w + n_ex]
    o0 = 1 + n_w + n_ex + n_alias
    out_refs = refs[o0:o0 + n_out]
    acc_refs = refs[o0 + n_out:]
    a = a_ref[...]
    parts = [jnp.dot(a, w[...], preferred_element_type=F32) for w in w_refs]
    if nk == 1:
        epilogue(parts, ex_refs, out_refs)
        return
    k = pl.program_id(2)

    @pl.when(k == 0)
    def _():
        for acc, p in zip(acc_refs, parts):
            acc[...] = p

    @pl.when((k > 0) & (k < nk - 1))
    def _():
        for acc, p in zip(acc_refs, parts):
            acc[...] += p

    @pl.when(k == nk - 1)
    def _():
        epilogue([acc[...] + p for acc, p in zip(acc_refs, parts)], ex_refs, out_refs)


def _dense(name, a, w, layer, col_starts, n_cols, tm, tn, epilogue, outs, extras=(), nk=1, carried=()):
    m, kdim = a.shape
    assert w.shape[1] == kdim and kdim % nk == 0 and m % tm == 0 and n_cols % tn == 0
    tk = kdim // nk
    in_specs = [pl.BlockSpec((tm, tk), lambda i, j, k: (i, k))]
    for s in col_starts:
        assert s % tn == 0
        in_specs.append(pl.BlockSpec((None, tk, tn), lambda i, j, k, s=s: (layer, k, s // tn + j)))
    in_specs += [spec for _, spec in extras]
    carried = dict(carried)
    n_in = len(in_specs)
    aliases = {}
    for pos, (oi, arr) in enumerate(sorted(carried.items())):
        in_specs.append(pl.BlockSpec(memory_space=pl.ANY))
        aliases[n_in + pos] = oi
    out_specs, out_shape = [], []
    for o in outs:
        if isinstance(o, tuple):
            out_shape.append(o[0])
            out_specs.append(o[1])
        else:
            out_shape.append(jax.ShapeDtypeStruct((m, n_cols), o))
            out_specs.append(pl.BlockSpec((tm, tn), lambda i, j, k: (i, j)))
    scratch = [pltpu.VMEM((tm, tn), F32) for _ in col_starts] if nk > 1 else []
    est = 2 * (_nbytes((tm, tk), BF16) + len(col_starts) * _nbytes((tk, tn), BF16))
    est += 2 * sum(_nbytes((tm, tn), s.dtype) for s in out_shape)
    est += 2 * sum(_nbytes([d for d in spec.block_shape if d is not None], arr.dtype) for arr, spec in extras)
    est += (len(col_starts) * (2 if nk > 1 else 1) + 1) * _nbytes((tm, tn), F32)
    kern = functools.partial(_dense_kernel, n_w=len(col_starts), n_ex=len(extras), n_alias=len(carried),
                             n_out=len(outs), nk=nk, epilogue=epilogue)
    return pl.pallas_call(
        kern,
        grid=(m // tm, n_cols // tn, nk),
        in_specs=in_specs,
        out_specs=out_specs,
        out_shape=out_shape,
        scratch_shapes=scratch,
        input_output_aliases=aliases,
        compiler_params=_params(("parallel", "parallel", "arbitrary"), est),
        name=name,
    )(a, *([w] * len(col_starts)), *[arr for arr, _ in extras], *[arr for _, arr in sorted(carried.items())])


def _epi_cast(parts, ex, outs):
    outs[0][...] = parts[0].astype(outs[0].dtype)


def _epi_swiglu(parts, ex, outs):
    gate, up = parts
    outs[0][...] = (_silu(gate) * up).astype(outs[0].dtype)


def _epi_residual_half(parts, ex, outs):
    outs[0][...] = ex[0][...] + 0.5 * parts[0]


def _epi_residual(parts, ex, outs):
    outs[0][...] = ex[0][...] + parts[0]


def _epi_gate(parts, ex, outs):
    outs[0][...] = _sigmoid(parts[0] + ex[0][...]).astype(outs[0].dtype)


def _epi_kv(parts, ex, outs, mode):
    acc = parts[0]
    native, copy16 = outs
    width = native.shape[-1]
    if mode != "cast":
        ys = []
        for c in range(acc.shape[1] // width):
            y = _rms_rows(acc[:, c * width:(c + 1) * width], ex[0][...])
            if mode == "norm_rope":
                y = _rope(y, ex[1][...], ex[2][...], ex[3][...])
            ys.append(y)
        acc = jnp.concatenate(ys, axis=1)
    native[...] = acc.reshape(native.shape)
    copy16[...] = acc.astype(copy16.dtype)


def _merge_kernel(oa_ref, ob_ref, oc_ref, wa_ref, wb_ref, wc_ref, ga_ref, gb_ref, gc_ref, o_ref):
    acc = ga_ref[...].astype(F32) * jnp.dot(oa_ref[...], wa_ref[...], preferred_element_type=F32)
    acc += gb_ref[...].astype(F32) * jnp.dot(ob_ref[...], wb_ref[...], preferred_element_type=F32)
    acc += gc_ref[...].astype(F32) * jnp.dot(oc_ref[...], wc_ref[...], preferred_element_type=F32)
    o_ref[...] = acc.astype(o_ref.dtype)


def _merge(oa, ob, oc, w_branch, gates, layer):
    m, bw = oa.shape
    d = w_branch.shape[-1]
    tm = _tile(m, 1024)
    tn = _tile(d, 512)
    nj = d // tn
    a_spec = pl.BlockSpec((tm, bw), lambda i, j: (i, 0))
    in_specs = [a_spec, a_spec, a_spec]
    in_specs += [pl.BlockSpec((None, None, bw, tn), lambda i, j, b=b: (layer, b, 0, j)) for b in range(N_BRANCH)]
    in_specs += [pl.BlockSpec((tm, tn), lambda i, j, b=b: (i, b * nj + j)) for b in range(N_BRANCH)]
    est = 2 * N_BRANCH * (_nbytes((tm, bw), BF16) + _nbytes((bw, tn), BF16) + _nbytes((tm, tn), BF16))
    est += 2 * _nbytes((tm, tn), BF16) + 3 * _nbytes((tm, tn), F32)
    return pl.pallas_call(
        _merge_kernel,
        grid=(m // tm, nj),
        in_specs=in_specs,
        out_specs=pl.BlockSpec((tm, tn), lambda i, j: (i, j)),
        out_shape=jax.ShapeDtypeStruct((m, d), BF16),
        compiler_params=_params(("parallel", "parallel"), est),
        name="merge",
    )(oa, ob, oc, w_branch, w_branch, w_branch, gates, gates, gates)


def _logf_cumsum_kernel(*refs, past):
    if past:
        raw_ref, fb_ref, hist_ref, lf_ref, c_ref = refs
    else:
        raw_ref, fb_ref, lf_ref, c_ref = refs
    x = raw_ref[...] + fb_ref[...]
    lf = jnp.minimum(x, 0.0) - jnp.log(1.0 + jnp.exp(-jnp.abs(x)))
    lf_ref[...] = lf
    if past:
        lf = jnp.concatenate([hist_ref[...], lf], axis=0)
    c_ref[...] = _prefix_sum(lf, 0)


def _logf_cumsum(raw, f_b3, layer, hist=None):
    b, t, w = raw.shape
    past = 0 if hist is None else hist.shape[1]
    in_specs = [pl.BlockSpec((None, t, w), lambda i: (i, 0, 0)),
                pl.BlockSpec((None, 1, w), lambda i: (layer, 0, 0))]
    args = [raw, f_b3]
    if past:
        in_specs.append(pl.BlockSpec((None, past, w), lambda i: (i, 0, 0)))
        args.append(hist)
    est = 12 * _nbytes((t + past, w), F32)
    return pl.pallas_call(
        functools.partial(_logf_cumsum_kernel, past=past),
        grid=(b,),
        in_specs=in_specs,
        out_specs=[pl.BlockSpec((None, t, w), lambda i: (i, 0, 0)),
                   pl.BlockSpec((None, past + t, w), lambda i: (i, 0, 0))],
        out_shape=[jax.ShapeDtypeStruct((b, t, w), F32), jax.ShapeDtypeStruct((b, past + t, w), F32)],
        compiler_params=_params(("parallel",), est),
        name="logf_cumsum",
    )(*args)


_NT = (((1,), (1,)), ((), ()))


def _softmax_pv(pieces, values):
    m = functools.reduce(jnp.maximum, [jnp.max(s, axis=-1, keepdims=True) for s in pieces])
    ps = [jnp.exp2(s - m) for s in pieces]
    l = functools.reduce(jnp.add, [jnp.sum(p, axis=-1, keepdims=True) for p in ps])
    o = functools.reduce(jnp.add, [jnp.dot(p.astype(BF16), v, preferred_element_type=F32)
                                   for p, v in zip(ps, values)])
    return o / l


def _diff_lambda(lam_ref, lam_init):
    lv = lam_ref[...]
    return (jnp.exp(jnp.sum(lv[0:1] * lv[1:2], axis=-1, keepdims=True))
            - jnp.exp(jnp.sum(lv[2:3] * lv[3:4], axis=-1, keepdims=True)) + lam_init)


def _diff_q(x, gq_ref, tc_ref, ta_ref, tb_ref):
    y = _rope(_rms_rows(x.astype(F32), gq_ref[...]), tc_ref[...], ta_ref[...], tb_ref[...])
    return (y * (A_HD ** -0.5 * LOG2E)).astype(BF16)


def _diff_finish(o0, o1, lam, go_ref, lam_init):
    return (_rms_rows(o0 - lam * o1, go_ref[...]) * (1.0 - lam_init)).astype(BF16)


def _diff_attn_prompt_kernel(q_ref, k_ref, vlo_ref, vhi_ref, tc_ref, ta_ref, tb_ref, gq_ref, lam_ref, go_ref,
                             o_ref, v_s, *, bq, lam_init):
    t = q_ref.shape[0]
    v_s[:, :A_HD] = vlo_ref[...]
    v_s[:, A_HD:] = vhi_ref[...]
    qs = [_diff_q(q_ref[:, c * A_HD:(c + 1) * A_HD], gq_ref, tc_ref, ta_ref, tb_ref) for c in range(2)]
    lam = _diff_lambda(lam_ref, lam_init)
    r = lax.broadcasted_iota(jnp.int32, (bq, bq), 0)
    j = lax.broadcasted_iota(jnp.int32, (bq, bq), 1)
    mask = j // CHUNK <= r // CHUNK
    for r0 in range(0, t, bq):
        n = r0 + bq
        comps = []
        for c in range(2):
            s = lax.dot_general(qs[c][r0:n], k_ref[:n, c * A_HD:(c + 1) * A_HD], _NT, preferred_element_type=F32)
            pieces = [jnp.where(mask, s[:, r0:], NEG_INF)]
            values = [v_s[r0:n, :]]
            if r0:
                pieces.insert(0, s[:, :r0])
                values.insert(0, v_s[:r0, :])
            comps.append(_softmax_pv(pieces, values))
        o_ref[r0:n, :] = _diff_finish(comps[0], comps[1], lam, go_ref, lam_init)


def _diff_attn_prompt(q, k16, v16, tabs, gq3, lam3, go3, layer, lam_init):
    b, t, w = q.shape
    hw = 2 * A_HD
    bq = min(ATTN_BLOCK, t)
    hspec = pl.BlockSpec((None, t, hw), lambda bi, h: (bi, 0, h))
    tspec = pl.BlockSpec((t, LANES), lambda bi, h: (0, 0))
    in_specs = [hspec, hspec,
                pl.BlockSpec((None, t, A_HD), lambda bi, h: (bi, 0, h)),
                pl.BlockSpec((None, t, A_HD), lambda bi, h: (bi, 0, A_HEADS + h)),
                tspec, tspec, tspec,
                pl.BlockSpec((None, 1, A_HD), lambda bi, h: (layer, 0, 0)),
                pl.BlockSpec((None, 4, A_HD), lambda bi, h: (layer, 0, 0)),
                pl.BlockSpec((None, 1, A_VD), lambda bi, h: (layer, 0, 0))]
    est = 10 * _nbytes((t, hw), BF16) + 6 * _nbytes((t, LANES), F32) + 8 * _nbytes((bq, t), F32)
    return pl.pallas_call(
        functools.partial(_diff_attn_prompt_kernel, bq=bq, lam_init=lam_init),
        grid=(b, A_HEADS),
        in_specs=in_specs,
        out_specs=hspec,
        out_shape=jax.ShapeDtypeStruct((b, t, w), BF16),
        scratch_shapes=[pltpu.VMEM((t, hw), BF16)],
        compiler_params=_params(("parallel", "parallel"), est),
        name="diff_attn_prompt",
    )(q, k16, v16, v16, *tabs, gq3, lam3, go3)


def _diff_attn_sample_kernel(q_ref, kc_ref, vc_ref, kn_ref, vn_ref, tc_ref, ta_ref, tb_ref, gq_ref, lam_ref,
                             go_ref, o_ref, *, past, lam_init):
    tq = q_ref.shape[0]
    lam = _diff_lambda(lam_ref, lam_init)
    qchunk = (past + lax.broadcasted_iota(jnp.int32, (tq, past), 0)) // CHUNK
    mask_c = lax.broadcasted_iota(jnp.int32, (tq, past), 1) // CHUNK <= qchunk
    qchunk = (past + lax.broadcasted_iota(jnp.int32, (tq, tq), 0)) // CHUNK
    mask_n = (past + lax.broadcasted_iota(jnp.int32, (tq, tq), 1)) // CHUNK <= qchunk
    half = A_HEADS * A_HD
    kc16 = [_token_major(kc_ref[:, g * SUBLANES:(g + 1) * SUBLANES, :]) for g in range(2 * A_HEADS // SUBLANES)]
    vc16 = [_token_major(vc_ref[:, :, s * A_HD:(s + 1) * A_HD]) for s in range(2)]
    for h in range(A_HEADS):
        vc = jnp.concatenate([v[:, h * A_HD:(h + 1) * A_HD] for v in vc16], axis=1)
        vn = jnp.concatenate([vn_ref[:, h * A_HD:(h + 1) * A_HD],
                              vn_ref[:, half + h * A_HD:half + (h + 1) * A_HD]], axis=1)
        comps = []
        for c in range(2):
            col = (2 * h + c) * A_HD
            q = _diff_q(q_ref[:, col:col + A_HD], gq_ref, tc_ref, ta_ref, tb_ref)
            krow = 2 * h + c
            kc = kc16[krow // SUBLANES][:, (krow % SUBLANES) * A_HD:(krow % SUBLANES + 1) * A_HD]
            sc = lax.dot_general(q, kc, _NT, preferred_element_type=F32)
            sn = lax.dot_general(q, kn_ref[:, col:col + A_HD], _NT, preferred_element_type=F32)
            comps.append(_softmax_pv([jnp.where(mask_c, sc, NEG_INF), jnp.where(mask_n, sn, NEG_INF)], [vc, vn]))
        o_ref[:, h * A_VD:(h + 1) * A_VD] = _diff_finish(comps[0], comps[1], lam, go_ref, lam_init)


def _diff_attn_sample(q, kc, vc, kn16, vn16, tabs, gq3, lam3, go3, layer, lam_init):
    b, t, w = q.shape
    past = kc.shape[2]
    row = pl.BlockSpec((None, t, w), lambda bi: (bi, 0, 0))
    tspec = pl.BlockSpec((t, LANES), lambda bi: (0, 0))
    in_specs = [row,
                pl.BlockSpec((None, None, past, 2 * A_HEADS, A_HD), lambda bi: (layer, bi, 0, 0, 0)),
                pl.BlockSpec((None, None, past, A_HEADS, A_VD), lambda bi: (layer, bi, 0, 0, 0)),
                row, row, tspec, tspec, tspec,
                pl.BlockSpec((None, 1, A_HD), lambda bi: (layer, 0, 0)),
                pl.BlockSpec((None, 4, A_HD), lambda bi: (layer, 0, 0)),
                pl.BlockSpec((None, 1, A_VD), lambda bi: (layer, 0, 0))]
    est = 4 * _nbytes((past, w), F32) + 16 * _nbytes((past, A_VD), F32)
    return pl.pallas_call(
        functools.partial(_diff_attn_sample_kernel, past=past, lam_init=lam_init),
        grid=(b,),
        in_specs=in_specs,
        out_specs=row,
        out_shape=jax.ShapeDtypeStruct((b, t, w), BF16),
        compiler_params=_params(("parallel",), est),
        name="diff_attn_sample",
    )(q, kc, vc, kn16, vn16, *tabs, gq3, lam3, go3)


def _fox_q(x, gq_ref):
    return (_rms_rows(x.astype(F32), gq_ref[...]) * (F_HD ** -0.5 * LOG2E)).astype(BF16)


def _fox_softmax_pv(pieces, cq, values):
    mt = functools.reduce(jnp.maximum, [jnp.max(s, axis=-1, keepdims=True) for s in pieces])
    shift = (mt + cq) - cq
    ps = [jnp.exp2(s - shift) for s in pieces]
    l = functools.reduce(jnp.add, [jnp.sum(p, axis=-1, keepdims=True) for p in ps])
    o = functools.reduce(jnp.add, [jnp.dot(p.astype(BF16), v, preferred_element_type=F32)
                                   for p, v in zip(ps, values)])
    return o / l


def _fox_attn_prompt_kernel(q_ref, k_ref, v_ref, cq_ref, ck_ref, gq_ref, o_ref, *, bq):
    t = q_ref.shape[0]
    h = pl.program_id(1)
    q = _fox_q(q_ref[...], gq_ref)
    cq = _lane_pick(cq_ref[...], h) * LOG2E
    ck = ck_ref[...] * LOG2E
    r = lax.broadcasted_iota(jnp.int32, (bq, bq), 0)
    j = lax.broadcasted_iota(jnp.int32, (bq, bq), 1)
    mask = j <= r
    for r0 in range(0, t, bq):
        n = r0 + bq
        s = lax.dot_general(q[r0:n], k_ref[:n, :], _NT, preferred_element_type=F32) - ck[:, :n]
        pieces = [jnp.where(mask, s[:, r0:], NEG_INF)]
        values = [v_ref[r0:n, :]]
        if r0:
            pieces.insert(0, s[:, :r0])
            values.insert(0, v_ref[:r0, :])
        o_ref[r0:n, :] = _fox_softmax_pv(pieces, cq[r0:n], values).astype(o_ref.dtype)


def _fox_attn_prompt(q, k16, v16, c_col, c_row, gq3, layer):
    b, t, w = q.shape
    bq = min(ATTN_BLOCK, t)
    hspec = pl.BlockSpec((None, t, F_HD), lambda bi, h: (bi, 0, h))
    in_specs = [hspec, hspec, hspec,
                pl.BlockSpec((None, t, F_HEADS), lambda bi, h: (bi, 0, 0)),
                pl.BlockSpec((None, None, 1, t), lambda bi, h: (bi, h, 0, 0)),
                pl.BlockSpec((None, 1, F_HD), lambda bi, h: (layer, 0, 0))]
    est = 8 * _nbytes((t, F_HD), BF16) + 4 * _nbytes((t, LANES), F32) + 8 * _nbytes((bq, t), F32)
    return pl.pallas_call(
        functools.partial(_fox_attn_prompt_kernel, bq=bq),
        grid=(b, F_HEADS),
        in_specs=in_specs,
        out_specs=hspec,
        out_shape=jax.ShapeDtypeStruct((b, t, w), BF16),
        compiler_params=_params(("parallel", "parallel"), est),
        name="fox_attn_prompt",
    )(q, k16, v16, c_col, c_row, gq3)


def _fox_attn_sample_kernel(q_ref, kc_ref, vc_ref, kn_ref, vn_ref, cq_ref, ck_ref, gq_ref, o_ref, *, past):
    tq = q_ref.shape[0]
    nh = kc_ref.shape[1]
    head0 = pl.program_id(1) * nh
    r = lax.broadcasted_iota(jnp.int32, (tq, tq), 0)
    j = lax.broadcasted_iota(jnp.int32, (tq, tq), 1)
    mask = j <= r
    kc16 = _token_major(kc_ref[...])
    vc16 = _token_major(vc_ref[...])
    for hh in range(nh):
        cols = slice(hh * F_HD, (hh + 1) * F_HD)
        q = _fox_q(q_ref[:, cols], gq_ref)
        cq = _lane_pick(cq_ref[...], head0 + hh) * LOG2E
        ck = ck_ref[hh] * LOG2E
        sc = lax.dot_general(q, kc16[:, cols], _NT, preferred_element_type=F32) - ck[:, :past]
        sn = lax.dot_general(q, kn_ref[:, cols], _NT, preferred_element_type=F32) - ck[:, past:]
        o = _fox_softmax_pv([sc, jnp.where(mask, sn, NEG_INF)], cq, [vc16[:, cols], vn_ref[:, cols]])
        o_ref[:, cols] = o.astype(o_ref.dtype)


def _fox_attn_sample(q, kc, vc, kn16, vn16, c_new, c_row, gq3, layer):
    b, t, w = q.shape
    past = kc.shape[2]
    nh = SUBLANES
    gw = nh * F_HD
    row = pl.BlockSpec((None, t, gw), lambda bi, g: (bi, 0, g))
    cspec = pl.BlockSpec((None, None, past, nh, F_HD), lambda bi, g: (layer, bi, 0, g, 0))
    in_specs = [row, cspec, cspec, row, row,
                pl.BlockSpec((None, t, F_HEADS), lambda bi, g: (bi, 0, 0)),
                pl.BlockSpec((None, nh, 1, past + t), lambda bi, g: (bi, g, 0, 0)),
                pl.BlockSpec((None, 1, F_HD), lambda bi, g: (layer, 0, 0))]
    est = 4 * _nbytes((past, gw), F32) + 16 * _nbytes((past, A_VD), F32)
    return pl.pallas_call(
        functools.partial(_fox_attn_sample_kernel, past=past),
        grid=(b, F_HEADS // nh),
        in_specs=in_specs,
        out_specs=row,
        out_shape=jax.ShapeDtypeStruct((b, t, w), BF16),
        compiler_params=_params(("parallel", "parallel"), est),
        name="fox_attn_sample",
    )(q, kc, vc, kn16, vn16, c_new, c_row, gq3)


def _ssd_kernel(xs_ref, bm_ref, cm_ref, z_ref, dtc_ref, dtr_ref,
                cx_ref, cb_ref, cc_ref, h0_ref,
                wx_ref, wb_ref, wc_ref, bx_ref, bb_ref, bc_ref,
                dtb_c_ref, dtb_r_ref, al_c_ref, al_r_ref, d_ref, ng_ref,
                o_ref, h1_ref,
                pad_s, xs_s, bm_s, cm_s, h_s, cumr_s, *, t, tp, chunk):
    e_heads = M_GROUP_HEADS
    hd = M_HEAD_DIM
    nc = tp // chunk
    head0 = pl.program_id(1) * e_heads + DT_LANE0

    def conv(raw_ref, hist_ref, w_ref, b_ref, dst):
        wdt = raw_ref.shape[1]
        pad_s[SUBLANES - (M_CONV - 1):SUBLANES, :wdt] = hist_ref[...].astype(F32)
        pad_s[SUBLANES:SUBLANES + t, :wdt] = raw_ref[...].astype(F32)
        rb = min(t, 256)
        for r0 in range(0, t, rb):
            acc = jnp.zeros((rb, wdt), F32) + b_ref[...]
            for k in range(M_CONV):
                lo = r0 + SUBLANES - (M_CONV - 1) + k
                acc = acc + w_ref[k:k + 1, :] * pad_s[lo:lo + rb, :wdt]
            dst[r0:r0 + rb, :] = _silu(acc)
        if tp > t:
            dst[t:tp, :] = jnp.zeros((tp - t, wdt), F32)

    conv(xs_ref, cx_ref, wx_ref, bx_ref, xs_s)
    conv(bm_ref, cb_ref, wb_ref, bb_ref, bm_s)
    conv(cm_ref, cc_ref, wc_ref, bc_ref, cm_s)

    h_s[...] = h0_ref[...]
    a_c = -jnp.exp(al_c_ref[...])
    a_r = -jnp.exp(al_r_ref[...])
    causal = (lax.broadcasted_iota(jnp.int32, (chunk, chunk), 1)
              <= lax.broadcasted_iota(jnp.int32, (chunk, chunk), 0))
    lane = lax.broadcasted_iota(jnp.int32, (chunk, M_GROUP_W), 1)

    def spread(cols):
        out = cols[e_heads - 1]
        for e in range(e_heads - 2, -1, -1):
            out = jnp.where(lane[:cols[e].shape[0]] < (e + 1) * hd, cols[e], out)
        return out

    def body(c, carry):
        r0 = c * chunk if isinstance(c, int) else pl.multiple_of(c * chunk, chunk)
        rows = pl.ds(r0, chunk)
        dt_c = _softplus(dtc_ref[rows, :] + dtb_c_ref[...])
        dt_r = _softplus(dtr_ref[c] + dtb_r_ref[...])
        if tp > t:
            dt_c = jnp.where(r0 + lax.broadcasted_iota(jnp.int32, dt_c.shape, 0) < t, dt_c, 0.0)
            dt_r = jnp.where(r0 + lax.broadcasted_iota(jnp.int32, dt_r.shape, 1) < t, dt_r, 0.0)
        cum_c = _prefix_sum(dt_c * a_c, 0)
        cumr_s[...] = _prefix_sum(dt_r * a_r, 1)
        cum_cols = [_lane_pick(cum_c, head0 + e) for e in range(e_heads)]
        cum_rows = [cumr_s[pl.ds(head0 + e, 1), :] for e in range(e_heads)]
        xs = xs_s[rows, :]
        bm = bm_s[rows, :].astype(BF16)
        cm = cm_s[rows, :].astype(BF16)
        cum_x = spread(cum_cols)
        xdt = xs * spread([_lane_pick(dt_c, head0 + e) for e in range(e_heads)])
        xdt16 = xdt.astype(BF16)
        g = lax.dot_general(cm, bm, _NT, preferred_element_type=F32)
        ys = []
        for e in range(e_heads):
            seg = cum_cols[e] - cum_rows[e]
            mat = g * jnp.exp(jnp.where(causal, seg, -jnp.inf))
            ys.append(jnp.dot(mat.astype(BF16), xdt16[:, e * hd:(e + 1) * hd], preferred_element_type=F32))
        y = jnp.concatenate(ys, axis=1)
        h = h_s[...]
        y = y + jnp.exp(cum_x) * lax.dot_general(cm, h.astype(BF16), _NT, preferred_element_type=F32)
        to_end = jnp.exp(cum_x[chunk - 1:chunk, :] - cum_x)
        st = lax.dot_general((xdt * to_end).astype(BF16), bm, (((0,), (0,)), ((), ())),
                             preferred_element_type=F32)
        for e in range(e_heads):
            dec = jnp.exp(cum_rows[e][:, chunk - 1:chunk])
            h_s[e * hd:(e + 1) * hd, :] = h[e * hd:(e + 1) * hd, :] * dec + st[e * hd:(e + 1) * hd, :]
        d_x = spread([_lane_pick(d_ref[...], head0 + e) for e in range(e_heads)])
        y = y + d_x * xs
        if tp > t:
            y = y[:t] * _silu(z_ref[...].astype(F32))
            o_ref[...] = _rms_rows(y, ng_ref[...]).astype(o_ref.dtype)
        else:
            y = y * _silu(z_ref[rows, :].astype(F32))
            o_ref[rows, :] = _rms_rows(y, ng_ref[...]).astype(o_ref.dtype)
        return carry

    if nc == 1:
        body(0, 0)
    else:
        assert tp == t
        lax.fori_loop(0, nc, body, 0)
    h1_ref[...] = h_s[...]


def _ssd(xbc, z, small, conv0, conv0_layer, ssm0, ssm0_layer, pr, layer):
    b, t, _ = xbc.shape
    chunk = min(SSD_CHUNK, -(-t // LANES) * LANES)
    tp = -(-t // chunk) * chunk
    nc = tp // chunk
    gw = M_GROUP_W
    ns = M_STATE
    b_off = M_INNER // ns
    c_off = (M_INNER + M_GROUPS * ns) // ns
    dt_col = jnp.pad(small, ((0, 0), (0, tp - t), (0, 0)))
    dt_row = dt_col.reshape(b, nc, chunk, SMALL_W).transpose(0, 1, 3, 2)

    def bspec(width, off):
        return pl.BlockSpec((None, t, width), lambda bi, g: (bi, 0, off + g))

    def hspec(width, off):
        return pl.BlockSpec((None, None, M_CONV - 1, width), lambda bi, g: (conv0_layer, bi, 0, off + g))

    def wspec(rows, width, off):
        return pl.BlockSpec((None, rows, width), lambda bi, g: (layer, 0, off + g))

    col_p = pl.BlockSpec((None, 1, SMALL_W), lambda bi, g: (layer, 0, 0))
    row_p = pl.BlockSpec((None, SMALL_W, 1), lambda bi, g: (layer, 0, 0))
    in_specs = [bspec(gw, 0), bspec(ns, b_off), bspec(ns, c_off), bspec(gw, 0),
                pl.BlockSpec((None, tp, SMALL_W), lambda bi, g: (bi, 0, 0)),
                pl.BlockSpec((None, nc, SMALL_W, chunk), lambda bi, g: (bi, 0, 0, 0)),
                hspec(gw, 0), hspec(ns, b_off), hspec(ns, c_off),
                pl.BlockSpec((None, None, gw, ns), lambda bi, g: (ssm0_layer, bi, g, 0)),
                wspec(M_CONV, gw, 0), wspec(M_CONV, ns, b_off), wspec(M_CONV, ns, c_off),
                wspec(1, gw, 0), wspec(1, ns, b_off), wspec(1, ns, c_off),
                col_p, row_p, col_p, row_p, col_p,
                wspec(1, gw, 0)]
    out_specs = [bspec(gw, 0), pl.BlockSpec((None, gw, ns), lambda bi, g: (bi, g, 0))]
    out_shape = [jax.ShapeDtypeStruct((b, t, M_INNER), BF16),
                 jax.ShapeDtypeStruct((b, M_HEADS * M_HEAD_DIM, ns), F32)]
    scratch = [pltpu.VMEM((t + SUBLANES, gw), F32), pltpu.VMEM((tp, gw), F32), pltpu.VMEM((tp, ns), F32),
               pltpu.VMEM((tp, ns), F32), pltpu.VMEM((gw, ns), F32), pltpu.VMEM((SMALL_W, chunk), F32)]
    est = 2 * (3 * _nbytes((t, gw), BF16) + 2 * _nbytes((t, ns), BF16) + 2 * _nbytes((tp, SMALL_W), F32))
    est += 2 * _nbytes((tp + SUBLANES, gw), F32) + 2 * _nbytes((tp, ns), F32)
    est += 32 * _nbytes((chunk, max(chunk, gw)), F32)
    return pl.pallas_call(
        functools.partial(_ssd_kernel, t=t, tp=tp, chunk=chunk),
        grid=(b, M_GROUPS),
        in_specs=in_specs,
        out_specs=out_specs,
        out_shape=out_shape,
        scratch_shapes=scratch,
        compiler_params=_params(("parallel", "parallel"), est),
        name="ssd",
    )(xbc, xbc, xbc, z, dt_col, dt_row, conv0, conv0, conv0, ssm0,
      pr["conv_w"], pr["conv_w"], pr["conv_w"], pr["conv_b"], pr["conv_b"], pr["conv_b"],
      pr["dt_bias_c"], pr["dt_bias_r"], pr["a_log_c"], pr["a_log_r"], pr["d_c"], pr["m_norm"])


def _rope_tables(pos):
    rot = A_HD // ROPE_FRACTION
    half = rot // 2
    inv_freq = jnp.float32(ROPE_THETA) ** (-jnp.arange(half, dtype=F32) / half)
    ang = pos.astype(F32)[:, None] * inv_freq[None, :]
    cos, sin = jnp.cos(ang), jnp.sin(ang)
    n = pos.shape[0]
    tc = jnp.concatenate([cos, cos, jnp.ones((n, A_HD - rot), F32)], axis=1)
    ta = jnp.concatenate([-sin, jnp.zeros((n, A_HD - half), F32)], axis=1)
    tb = jnp.concatenate([jnp.zeros((n, half), F32), sin, jnp.zeros((n, A_HD - rot), F32)], axis=1)
    return tc, ta, tb


def _lanes(p, lane0):
    depth, n = p.shape
    return jnp.pad(p, ((0, 0), (lane0, SMALL_W - lane0 - n))).reshape(depth, 1, SMALL_W)


def _layer(x, st, past, wts, pr, tabs_seq, layer, depth, conv0_layer, ssm0_layer, kv_prev):
    b, t, d = x.shape
    m = b * t
    a_kc, a_vc, conv0, ssm0, f_kc, f_vc, f_lfc = st
    lam_init = 0.8 - 0.6 * math.exp(-0.3 * layer)
    d_ff = wts["ffn1_out"].shape[1]
    tm = _tile(m, 1024)
    x2 = x.reshape(m, d)

    def ffn(x2, norm_g, w_in, w_out):
        hn = _rmsnorm(x2, norm_g, layer)
        tn = _tile(d_ff, 512)
        (hid,) = _dense("ffn_up", hn, w_in, layer, (0, d_ff), d_ff, tm, tn, _epi_swiglu, (BF16,))
        tn = _tile(d, 512)
        res = (x2, pl.BlockSpec((tm, tn), lambda i, j, k: (i, j)))
        (y,) = _dense("ffn_down", hid, w_out, layer, (0,), d, tm, tn, _epi_residual_half, (F32,),
                      extras=(res,), nk=2)
        return y

    x2 = ffn(x2, pr["norm_ffn1"], wts["ffn1_in"], wts["ffn1_out"])

    hn = _rmsnorm(x2, pr["norm_mix"], layer)
    w_in = wts["w_in"]
    seg = wts["seg"]
    aw = A_HEADS * 2 * A_HD
    fw = F_HEADS * F_HD

    def proj(name, key, width, epi, dtype, extras=()):
        tn = _tile(width, 1024, seg[key])
        (o,) = _dense(name, hn, w_in, layer, (seg[key],), width, tm, tn, epi, (dtype,), extras=extras)
        return o

    nb, tt = (1, tm) if tm <= t else (tm // t, t)
    nt = t // tt
    ktn = KV_ROWS * KV_W

    def proj_kv(name, key, mode, slot, extras=(), wide_rows=False):
        if wide_rows:
            native = (jax.ShapeDtypeStruct((depth, b, t, KV_ROWS, aw // KV_ROWS), F32),
                      pl.BlockSpec((None, nb, tt, KV_ROWS, KV_W),
                                   lambda i, j, k: (layer, i // nt, i % nt, 0, j)))
        else:
            native = (jax.ShapeDtypeStruct((depth, b, t, aw // KV_W, KV_W), F32),
                      pl.BlockSpec((None, nb, tt, KV_ROWS, KV_W),
                                   lambda i, j, k: (layer, i // nt, i % nt, j, 0)))
        copy16 = (jax.ShapeDtypeStruct((m, aw), BF16), pl.BlockSpec((tm, ktn), lambda i, j, k: (i, j)))
        carried = {} if kv_prev is None else {0: kv_prev[slot]}
        return _dense(name, hn, w_in, layer, (seg[key],), aw, tm, ktn, functools.partial(_epi_kv, mode=mode),
                      (native, copy16), extras=extras, carried=carried)

    if nb == 1:
        tab_specs = [(tb_, pl.BlockSpec((tm, LANES), lambda i, j, k: (i % nt, 0))) for tb_ in tabs_seq]
    else:
        tab_specs = [(jnp.tile(tb_, (nb, 1)), pl.BlockSpec((tm, LANES), lambda i, j, k: (0, 0)))
                     for tb_ in tabs_seq]

    def gain_spec(arr):
        return (arr, pl.BlockSpec((None, 1, arr.shape[-1]), lambda i, j, k: (layer, 0, 0)))

    aq = proj("proj_aq", "aq", aw, _epi_cast, BF16)
    ak_n, ak16 = proj_kv("proj_ak", "ak", "norm_rope", 0, extras=(gain_spec(pr["a_k_norm"]), *tab_specs))
    av_n, av16 = proj_kv("proj_av", "av", "cast", 1, wide_rows=True)
    mz = proj("proj_mz", "mz", M_INNER, _epi_cast, BF16)
    mxbc = proj("proj_mxbc", "mxbc", M_CONV_CH, _epi_cast, BF16)
    fq = proj("proj_fq", "fq", fw, _epi_cast, BF16)
    fk_n, fk16 = proj_kv("proj_fk", "fk", "norm", 2, extras=(gain_spec(pr["f_k_norm"]),))
    fv_n, fv16 = proj_kv("proj_fv", "fv", "cast", 3)
    gtn = _tile(N_BRANCH * d, 1024, seg["gt"])
    bias = (pr["b_gate"], pl.BlockSpec((None, 1, gtn), lambda i, j, k: (layer, 0, j)))
    gates = proj("proj_gate", "gt", N_BRANCH * d, _epi_gate, BF16, extras=(bias,))
    small = proj("proj_small", "small", SMALL_W, _epi_cast, F32).reshape(b, t, SMALL_W)

    a_args = (tabs_seq, pr["a_q_norm"], pr["a_lambda"], pr["a_out_norm"], layer, lam_init)
    aq3, ak3, av3 = aq.reshape(b, t, aw), ak16.reshape(b, t, aw), av16.reshape(b, t, aw)
    if past:
        oa = _diff_attn_sample(aq3, a_kc, a_vc, ak3, av3, *a_args)
    else:
        oa = _diff_attn_prompt(aq3, ak3, av3, *a_args)

    xbc3 = mxbc.reshape(b, t, M_CONV_CH)
    ob, ssm1 = _ssd(xbc3, mz.reshape(b, t, M_INNER), small, conv0, conv0_layer, ssm0, ssm0_layer, pr, layer)
    conv1 = jnp.concatenate([conv0[conv0_layer], xbc3[:, t - min(t, M_CONV - 1):].astype(F32)],
                            axis=1)[:, -(M_CONV - 1):]

    ff_lanes = slice(FF_LANE0, FF_LANE0 + F_HEADS)
    hist = None
    if past:
        hist = jnp.pad(f_lfc[layer], ((0, 0), (0, 0), (FF_LANE0, SMALL_W - FF_LANE0 - F_HEADS)))
    lf_w, c_w = _logf_cumsum(small, pr["f_b"], layer, hist)
    lf = lf_w[:, :, ff_lanes]
    c_all = c_w[:, :, ff_lanes]
    c_new = c_all[:, past:]
    c_row = c_all.transpose(0, 2, 1).reshape(b, F_HEADS, 1, past + t)
    fq3, fk3, fv3 = fq.reshape(b, t, fw), fk16.reshape(b, t, fw), fv16.reshape(b, t, fw)
    if past:
        oc = _fox_attn_sample(fq3, f_kc, f_vc, fk3, fv3, c_new, c_row, pr["f_q_norm"], layer)
    else:
        oc = _fox_attn_prompt(fq3, fk3, fv3, c_new, c_row, pr["f_q_norm"], layer)

    merged = _merge(oa.reshape(m, BRANCH_W), ob.reshape(m, BRANCH_W), oc.reshape(m, BRANCH_W),
                    wts["w_branch"], gates, layer)
    tn = _tile(d, 512)
    res = (x2, pl.BlockSpec((tm, tn), lambda i, j, k: (i, j)))
    (x2,) = _dense("out_proj", merged, wts["w_out"], layer, (0,), d, tm, tn, _epi_residual, (F32,), extras=(res,))

    x2 = ffn(x2, pr["norm_ffn2"], wts["ffn2_in"], wts["ffn2_out"])

    small_new = (conv1, ssm1.reshape(b, M_HEADS, M_HEAD_DIM, M_STATE), lf)
    return x2.reshape(b, t, d), (ak_n, av_n, fk_n, fv_n), small_new


def _prep_w_in(w_in, d):
    aw = A_HEADS * 2 * A_HD
    fw = F_HEADS * F_HD
    sizes = (aw, aw, A_HEADS * A_VD, M_INNER, M_CONV_CH, M_HEADS, fw, fw, fw, F_HEADS, N_BRANCH * d)
    names = ("aq", "ak", "av", "mz", "mxbc", "mdt", "fq", "fk", "fv", "ff", "gt")
    assert sum(sizes) == w_in.shape[-1]
    start, src = 0, {}
    for n, s in zip(names, sizes):
        src[n] = (start, s)
        start += s
    order = ("aq", "ak", "av", "mz", "mxbc", "fq", "fk", "fv", "gt")
    seg, pieces, off = {}, [], 0
    for n in order:
        s0, s = src[n]
        seg[n] = off
        piece = w_in[:, :, s0:s0 + s]
        if n == "av":
            piece = piece.reshape(piece.shape[:2] + (A_HEADS, 2, A_HD)).swapaxes(2, 3).reshape(piece.shape)
        pieces.append(piece)
        off += s
    seg["small"] = off
    assert DT_LANE0 == 0 and FF_LANE0 == M_HEADS
    pieces.append(w_in[:, :, src["mdt"][0]:src["mdt"][0] + M_HEADS])
    pieces.append(w_in[:, :, src["ff"][0]:src["ff"][0] + F_HEADS])
    pieces.append(jnp.zeros(w_in.shape[:2] + (SMALL_W - M_HEADS - F_HEADS,), w_in.dtype))
    return jnp.concatenate(pieces, axis=-1).astype(BF16), seg


def kernel(x_prompt, x_sample, cache_a_k, cache_a_v, state_conv, state_ssm, cache_f_k, cache_f_v, cache_f_logf, norm_ffn1, w_ffn1_in, w_ffn1_out, norm_mix, w_in, b_gate, a_q_norm, a_k_norm, a_lambda, a_out_norm, m_conv_w, m_conv_b, m_dt_bias, m_a_log, m_d, m_norm, f_q_norm, f_k_norm, f_b, w_branch, w_out, norm_ffn2, w_ffn2_in, w_ffn2_out):
    depth = w_in.shape[0]
    bp, tp, d = x_prompt.shape
    bs, ts, _ = x_sample.shape
    past = cache_a_k.shape[2]

    w_in16, seg = _prep_w_in(w_in, d)
    wts = dict(ffn1_in=w_ffn1_in.astype(BF16), ffn1_out=w_ffn1_out.astype(BF16), w_in=w_in16, seg=seg,
               w_branch=w_branch.astype(BF16), w_out=w_out.astype(BF16),
               ffn2_in=w_ffn2_in.astype(BF16), ffn2_out=w_ffn2_out.astype(BF16))

    def row3(p):
        return p.reshape(depth, 1, p.shape[-1])

    dt_bias_w, a_log_w = _lanes(m_dt_bias, DT_LANE0), _lanes(m_a_log, DT_LANE0)
    pr = dict(norm_ffn1=row3(norm_ffn1), norm_mix=row3(norm_mix), norm_ffn2=row3(norm_ffn2),
              b_gate=b_gate.reshape(depth, 1, N_BRANCH * d),
              a_q_norm=row3(a_q_norm), a_k_norm=row3(a_k_norm), a_lambda=a_lambda, a_out_norm=row3(a_out_norm),
              conv_w=m_conv_w, conv_b=row3(m_conv_b), m_norm=row3(m_norm),
              dt_bias_c=dt_bias_w, dt_bias_r=dt_bias_w.reshape(depth, SMALL_W, 1),
              a_log_c=a_log_w, a_log_r=a_log_w.reshape(depth, SMALL_W, 1),
              d_c=_lanes(m_d, DT_LANE0),
              f_q_norm=row3(f_q_norm), f_k_norm=row3(f_k_norm), f_b=_lanes(f_b, FF_LANE0))

    tabs_p = _rope_tables(jnp.arange(tp, dtype=jnp.int32))
    tabs_s = _rope_tables(past + jnp.arange(ts, dtype=jnp.int32))

    st_p = (None, None, jnp.zeros((1, bp, M_CONV - 1, M_CONV_CH), F32),
            jnp.zeros((1, bp, M_HEADS * M_HEAD_DIM, M_STATE), F32), None, None, None)
    st_s = (cache_a_k.reshape(depth, bs, past, 2 * A_HEADS, A_HD), cache_a_v,
            state_conv, state_ssm.reshape(depth, bs, M_HEADS * M_HEAD_DIM, M_STATE),
            cache_f_k, cache_f_v, cache_f_logf)

    y_p, y_s = x_prompt, x_sample
    kv_p = kv_s = None
    new_p, new_s = [], []
    for layer in range(depth):
        y_p, kv_p, n_p = _layer(y_p, st_p, 0, wts, pr, tabs_p, layer, depth, 0, 0, kv_p)
        y_s, kv_s, n_s = _layer(y_s, st_s, past, wts, pr, tabs_s, layer, depth, layer, layer, kv_s)
        new_p.append(n_p)
        new_s.append(n_s)

    def stack(rows, i):
        return jnp.stack([r[i] for r in rows], axis=0)

    def group_out(y, kv, new, b, t):
        ak_n, av_n, fk_n, fv_n = kv
        return (ak_n.reshape(depth, b, t, A_HEADS, 2, A_HD), av_n, stack(new, 0), stack(new, 1),
                fk_n, fv_n, stack(new, 2))

    out_p = group_out(y_p, kv_p, new_p, bp, tp)
    out_s = group_out(y_s, kv_s, new_s, bs, ts)
    return (y_p, y_s) + out_p + out_s
```

```python
import functools
import math

import jax
import jax.numpy as jnp
from jax import lax
from jax.experimental import pallas as pl
from jax.experimental.pallas import tpu as pltpu

F32 = jnp.float32
BF16 = jnp.bfloat16

CHUNK = 64
N_BRANCH = 3
BRANCH_W = 2048
A_HEADS = 8
A_HD = 128
A_VD = 2 * A_HD
M_INNER = 2048
M_HEAD_DIM = 64
M_HEADS = M_INNER // M_HEAD_DIM
M_GROUPS = 8
M_GROUP_HEADS = M_HEADS // M_GROUPS
M_GROUP_W = M_INNER // M_GROUPS
M_STATE = 128
M_CONV = 4
M_CONV_CH = M_INNER + 2 * M_GROUPS * M_STATE
F_HEADS = 16
F_HD = 128
ROPE_THETA = 500000.0
ROPE_FRACTION = 4
EPS = 1e-6
NEG_INF = -1e30
LOG2E = 1.4426950408889634

LANES = 128
SUBLANES = 8
V7X_VMEM_BYTES = 64 * 1024 * 1024
VMEM_CAP = V7X_VMEM_BYTES - 6 * 1024 * 1024
VMEM_INTERNAL = 4 * 1024 * 1024

ATTN_BLOCK = 256
SSD_CHUNK = 256
SMALL_W = LANES
DT_LANE0 = 0
FF_LANE0 = M_HEADS
KV_ROWS = SUBLANES
KV_W = LANES

def _params(sem, est_bytes):
    return pltpu.CompilerParams(dimension_semantics=sem,
                                vmem_limit_bytes=int(min(VMEM_CAP, est_bytes + VMEM_INTERNAL)))


def _nbytes(shape, dtype):
    return math.prod(shape) * jnp.dtype(dtype).itemsize


def _tile(n, pref, align=0):
    t = pref
    while t > SUBLANES and (n % t or align % t):
        t //= 2
    assert n % t == 0 and align % t == 0
    return t


def _sigmoid(x):
    return 0.5 * (1.0 + jnp.tanh(0.5 * x))


def _silu(x):
    return x * _sigmoid(x)


def _softplus(x):
    return jnp.maximum(x, 0.0) + jnp.log(1.0 + jnp.exp(-jnp.abs(x)))


def _rms_rows(x, gain):
    return x * lax.rsqrt(jnp.mean(x * x, axis=-1, keepdims=True) + EPS) * gain


def _rope(y, tc, ta, tb):
    return y * tc + pltpu.roll(y, LANES - 16, 1) * ta + pltpu.roll(y, 16, 1) * tb


def _prefix_sum(x, axis):
    n = x.shape[axis]
    idx = lax.broadcasted_iota(jnp.int32, x.shape, axis)
    d = 1
    while d < n:
        x = x + jnp.where(idx >= d, pltpu.roll(x, d, axis), 0.0)
        d *= 2
    return x


def _lane_pick(blk, lane_index):
    lane = lax.broadcasted_iota(jnp.int32, blk.shape, 1)
    return jnp.sum(jnp.where(lane == lane_index, blk, 0.0), axis=-1, keepdims=True)


def _token_major(x):
    tokens, rows, width = x.shape
    return x.reshape(tokens, rows * width).astype(BF16)


def _rmsnorm_kernel(x_ref, g_ref, o_ref):
    o_ref[...] = _rms_rows(x_ref[...], g_ref[...]).astype(o_ref.dtype)


def _rmsnorm(x, gain3, layer):
    m, d = x.shape
    tm = _tile(m, 256)
    est = 2 * (_nbytes((tm, d), F32) + _nbytes((tm, d), BF16)) + 2 * _nbytes((tm, d), F32)
    return pl.pallas_call(
        _rmsnorm_kernel,
        grid=(m // tm,),
        in_specs=[pl.BlockSpec((tm, d), lambda i: (i, 0)),
                  pl.BlockSpec((None, 1, d), lambda i: (layer, 0, 0))],
        out_specs=pl.BlockSpec((tm, d), lambda i: (i, 0)),
        out_shape=jax.ShapeDtypeStruct((m, d), BF16),
        compiler_params=_params(("parallel",), est),
        name="rmsnorm",
    )(x, gain3)


def _dense_kernel(*refs, n_w, n_ex, n_alias, n_out, epilogue):
    a_ref = refs[0]
    w_refs = refs[1:1 + n_w]
    ex_refs = refs[1 + n_w:1 + n_w + n_ex]
    o0 = 1 + n_w + n_ex + n_alias
    out_refs = refs[o0:o0 + n_out]
    a = a_ref[...]
    epilogue([jnp.dot(a, w[...], preferred_element_type=F32) for w in w_refs], ex_refs, out_refs)


def _dense(name, a, w, layer, col_starts, n_cols, tm, tn, epilogue, outs, extras=(), carried=()):
    m, kdim = a.shape
    assert w.shape[1] == kdim and m % tm == 0 and n_cols % tn == 0
    in_specs = [pl.BlockSpec((tm, kdim), lambda i, j: (i, 0))]
    for s in col_starts:
        assert s % tn == 0
        in_specs.append(pl.BlockSpec((None, kdim, tn), lambda i, j, s=s: (layer, 0, s // tn + j)))
    in_specs += [spec for _, spec in extras]
    carried = dict(carried)
    n_in = len(in_specs)
    aliases = {}
    for pos, (oi, arr) in enumerate(sorted(carried.items())):
        in_specs.append(pl.BlockSpec(memory_space=pl.ANY))
        aliases[n_in + pos] = oi
    out_specs, out_shape = [], []
    for o in outs:
        if isinstance(o, tuple):
            out_shape.append(o[0])
            out_specs.append(o[1])
        else:
            out_shape.append(jax.ShapeDtypeStruct((m, n_cols), o))
            out_specs.append(pl.BlockSpec((tm, tn), lambda i, j: (i, j)))
    est = 2 * (_nbytes((tm, kdim), BF16) + len(col_starts) * _nbytes((kdim, tn), BF16))
    est += 2 * sum(_nbytes((tm, tn), s.dtype) for s in out_shape)
    est += 2 * sum(_nbytes([d for d in spec.block_shape if d is not None], arr.dtype) for arr, spec in extras)
    est += (len(col_starts) + 1) * _nbytes((tm, tn), F32)
    kern = functools.partial(_dense_kernel, n_w=len(col_starts), n_ex=len(extras), n_alias=len(carried),
                             n_out=len(outs), epilogue=epilogue)
    return pl.pallas_call(
        kern,
        grid=(m // tm, n_cols // tn),
        in_specs=in_specs,
        out_specs=out_specs,
        out_shape=out_shape,
        input_output_aliases=aliases,
        compiler_params=_params(("parallel", "parallel"), est),
        name=name,
    )(a, *([w] * len(col_starts)), *[arr for arr, _ in extras], *[arr for _, arr in sorted(carried.items())])


def _epi_cast(parts, ex, outs):
    outs[0][...] = parts[0].astype(outs[0].dtype)


def _epi_swiglu(parts, ex, outs):
    gate, up = parts
    outs[0][...] = (_silu(gate) * up).astype(outs[0].dtype)


def _epi_residual_half(parts, ex, outs):
    outs[0][...] = ex[0][...] + 0.5 * parts[0]


def _epi_residual(parts, ex, outs):
    outs[0][...] = ex[0][...] + parts[0]


def _epi_gate(parts, ex, outs):
    outs[0][...] = _sigmoid(parts[0] + ex[0][...]).astype(outs[0].dtype)


def _epi_kv(parts, ex, outs, mode):
    native, copy16 = outs
    width = native.shape[-1]
    acc = parts[0]
    if mode != "cast":
        ys = []
        for c in range(acc.shape[1] // width):
            y = _rms_rows(acc[:, c * width:(c + 1) * width], ex[0][...])
            if mode == "norm_rope":
                y = _rope(y, ex[1][...], ex[2][...], ex[3][...])
            ys.append(y)
        acc = jnp.concatenate(ys, axis=1)
    native[...] = acc.reshape(native.shape)
    copy16[...] = acc.astype(copy16.dtype)


def _merge_kernel(oa_ref, ob_ref, oc_ref, wa_ref, wb_ref, wc_ref, ga_ref, gb_ref, gc_ref, o_ref):
    acc = ga_ref[...].astype(F32) * jnp.dot(oa_ref[...], wa_ref[...], preferred_element_type=F32)
    acc += gb_ref[...].astype(F32) * jnp.dot(ob_ref[...], wb_ref[...], preferred_element_type=F32)
    acc += gc_ref[...].astype(F32) * jnp.dot(oc_ref[...], wc_ref[...], preferred_element_type=F32)
    o_ref[...] = acc.astype(o_ref.dtype)


def _merge(oa, ob, oc, w_branch, gates, layer):
    m, bw = oa.shape
    d = w_branch.shape[-1]
    tm = _tile(m, 1024)
    tn = _tile(d, 512)
    nj = d // tn
    a_spec = pl.BlockSpec((tm, bw), lambda i, j: (i, 0))
    in_specs = [a_spec, a_spec, a_spec]
    in_specs += [pl.BlockSpec((None, None, bw, tn), lambda i, j, b=b: (layer, b, 0, j)) for b in range(N_BRANCH)]
    in_specs += [pl.BlockSpec((tm, tn), lambda i, j, b=b: (i, b * nj + j)) for b in range(N_BRANCH)]
    est = 2 * N_BRANCH * (_nbytes((tm, bw), BF16) + _nbytes((bw, tn), BF16) + _nbytes((tm, tn), BF16))
    est += 2 * _nbytes((tm, tn), BF16) + 3 * _nbytes((tm, tn), F32)
    return pl.pallas_call(
        _merge_kernel,
        grid=(m // tm, nj),
        in_specs=in_specs,
        out_specs=pl.BlockSpec((tm, tn), lambda i, j: (i, j)),
        out_shape=jax.ShapeDtypeStruct((m, d), BF16),
        compiler_params=_params(("parallel", "parallel"), est),
        name="merge",
    )(oa, ob, oc, w_branch, w_branch, w_branch, gates, gates, gates)


def _logf_cumsum_kernel(*refs, past):
    if past:
        raw_ref, fb_ref, hist_ref, lf_ref, c_ref = refs
    else:
        raw_ref, fb_ref, lf_ref, c_ref = refs
    x = raw_ref[...] + fb_ref[...]
    lf = jnp.minimum(x, 0.0) - jnp.log(1.0 + jnp.exp(-jnp.abs(x)))
    lf_ref[...] = lf
    if past:
        lf = jnp.concatenate([hist_ref[...], lf], axis=0)
    c_ref[...] = _prefix_sum(lf, 0)


def _logf_cumsum(raw, f_b3, layer, hist=None):
    b, t, w = raw.shape
    past = 0 if hist is None else hist.shape[1]
    in_specs = [pl.BlockSpec((None, t, w), lambda i: (i, 0, 0)),
                pl.BlockSpec((None, 1, w), lambda i: (layer, 0, 0))]
    args = [raw, f_b3]
    if past:
        in_specs.append(pl.BlockSpec((None, past, w), lambda i: (i, 0, 0)))
        args.append(hist)
    est = 12 * _nbytes((t + past, w), F32)
    return pl.pallas_call(
        functools.partial(_logf_cumsum_kernel, past=past),
        grid=(b,),
        in_specs=in_specs,
        out_specs=[pl.BlockSpec((None, t, w), lambda i: (i, 0, 0)),
                   pl.BlockSpec((None, past + t, w), lambda i: (i, 0, 0))],
        out_shape=[jax.ShapeDtypeStruct((b, t, w), F32), jax.ShapeDtypeStruct((b, past + t, w), F32)],
        compiler_params=_params(("parallel",), est),
        name="logf_cumsum",
    )(*args)


_NT = (((1,), (1,)), ((), ()))


def _softmax_pv(pieces, values):
    m = functools.reduce(jnp.maximum, [jnp.max(s, axis=-1, keepdims=True) for s in pieces])
    ps = [jnp.exp2(s - m) for s in pieces]
    l = functools.reduce(jnp.add, [jnp.sum(p, axis=-1, keepdims=True) for p in ps])
    o = functools.reduce(jnp.add, [jnp.dot(p.astype(BF16), v, preferred_element_type=F32)
                                   for p, v in zip(ps, values)])
    return o / l


def _diff_lambda(lam_ref, lam_init):
    lv = lam_ref[...]
    return (jnp.exp(jnp.sum(lv[0:1] * lv[1:2], axis=-1, keepdims=True))
            - jnp.exp(jnp.sum(lv[2:3] * lv[3:4], axis=-1, keepdims=True)) + lam_init)


def _diff_q(x, gq_ref, tc_ref, ta_ref, tb_ref):
    y = _rope(_rms_rows(x.astype(F32), gq_ref[...]), tc_ref[...], ta_ref[...], tb_ref[...])
    return (y * (A_HD ** -0.5 * LOG2E)).astype(BF16)


def _diff_finish(o0, o1, lam, go_ref, lam_init):
    return (_rms_rows(o0 - lam * o1, go_ref[...]) * (1.0 - lam_init)).astype(BF16)


def _diff_attn_prompt_kernel(q_ref, k_ref, vlo_ref, vhi_ref, tc_ref, ta_ref, tb_ref, gq_ref, lam_ref, go_ref,
                             o_ref, v_s, *, bq, lam_init):
    t = q_ref.shape[0]
    v_s[:, :A_HD] = vlo_ref[...]
    v_s[:, A_HD:] = vhi_ref[...]
    qs = [_diff_q(q_ref[:, c * A_HD:(c + 1) * A_HD], gq_ref, tc_ref, ta_ref, tb_ref) for c in range(2)]
    lam = _diff_lambda(lam_ref, lam_init)
    r = lax.broadcasted_iota(jnp.int32, (bq, bq), 0)
    j = lax.broadcasted_iota(jnp.int32, (bq, bq), 1)
    mask = j // CHUNK <= r // CHUNK
    for r0 in range(0, t, bq):
        n = r0 + bq
        comps = []
        for c in range(2):
            s = lax.dot_general(qs[c][r0:n], k_ref[:n, c * A_HD:(c + 1) * A_HD], _NT, preferred_element_type=F32)
            pieces = [jnp.where(mask, s[:, r0:], NEG_INF)]
            values = [v_s[r0:n, :]]
            if r0:
                pieces.insert(0, s[:, :r0])
                values.insert(0, v_s[:r0, :])
            comps.append(_softmax_pv(pieces, values))
        o_ref[r0:n, :] = _diff_finish(comps[0], comps[1], lam, go_ref, lam_init)


def _diff_attn_prompt(q, k16, v16, tabs, gq3, lam3, go3, layer, lam_init):
    b, t, w = q.shape
    hw = 2 * A_HD
    bq = min(ATTN_BLOCK, t)
    hspec = pl.BlockSpec((None, t, hw), lambda bi, h: (bi, 0, h))
    tspec = pl.BlockSpec((t, LANES), lambda bi, h: (0, 0))
    in_specs = [hspec, hspec,
                pl.BlockSpec((None, t, A_HD), lambda bi, h: (bi, 0, h)),
                pl.BlockSpec((None, t, A_HD), lambda bi, h: (bi, 0, A_HEADS + h)),
                tspec, tspec, tspec,
                pl.BlockSpec((None, 1, A_HD), lambda bi, h: (layer, 0, 0)),
                pl.BlockSpec((None, 4, A_HD), lambda bi, h: (layer, 0, 0)),
                pl.BlockSpec((None, 1, A_VD), lambda bi, h: (layer, 0, 0))]
    est = 10 * _nbytes((t, hw), BF16) + 6 * _nbytes((t, LANES), F32) + 8 * _nbytes((bq, t), F32)
    return pl.pallas_call(
        functools.partial(_diff_attn_prompt_kernel, bq=bq, lam_init=lam_init),
        grid=(b, A_HEADS),
        in_specs=in_specs,
        out_specs=hspec,
        out_shape=jax.ShapeDtypeStruct((b, t, w), BF16),
        scratch_shapes=[pltpu.VMEM((t, hw), BF16)],
        compiler_params=_params(("parallel", "parallel"), est),
        name="diff_attn_prompt",
    )(q, k16, v16, v16, *tabs, gq3, lam3, go3)


def _diff_attn_sample_kernel(q_ref, kc_ref, vc_ref, kn_ref, vn_ref, tc_ref, ta_ref, tb_ref, gq_ref, lam_ref,
                             go_ref, o_ref, *, past, lam_init):
    tq = q_ref.shape[0]
    lam = _diff_lambda(lam_ref, lam_init)
    qchunk = (past + lax.broadcasted_iota(jnp.int32, (tq, past), 0)) // CHUNK
    mask_c = lax.broadcasted_iota(jnp.int32, (tq, past), 1) // CHUNK <= qchunk
    qchunk = (past + lax.broadcasted_iota(jnp.int32, (tq, tq), 0)) // CHUNK
    mask_n = (past + lax.broadcasted_iota(jnp.int32, (tq, tq), 1)) // CHUNK <= qchunk
    half = A_HEADS * A_HD
    kc16 = [_token_major(kc_ref[:, g * SUBLANES:(g + 1) * SUBLANES, :]) for g in range(2 * A_HEADS // SUBLANES)]
    vc16 = [_token_major(vc_ref[:, :, s * A_HD:(s + 1) * A_HD]) for s in range(2)]
    for h in range(A_HEADS):
        vc = jnp.concatenate([v[:, h * A_HD:(h + 1) * A_HD] for v in vc16], axis=1)
        vn = jnp.concatenate([vn_ref[:, h * A_HD:(h + 1) * A_HD],
                              vn_ref[:, half + h * A_HD:half + (h + 1) * A_HD]], axis=1)
        comps = []
        for c in range(2):
            col = (2 * h + c) * A_HD
            q = _diff_q(q_ref[:, col:col + A_HD], gq_ref, tc_ref, ta_ref, tb_ref)
            krow = 2 * h + c
            kc = kc16[krow // SUBLANES][:, (krow % SUBLANES) * A_HD:(krow % SUBLANES + 1) * A_HD]
            sc = lax.dot_general(q, kc, _NT, preferred_element_type=F32)
            sn = lax.dot_general(q, kn_ref[:, col:col + A_HD], _NT, preferred_element_type=F32)
            comps.append(_softmax_pv([jnp.where(mask_c, sc, NEG_INF), jnp.where(mask_n, sn, NEG_INF)], [vc, vn]))
        o_ref[:, h * A_VD:(h + 1) * A_VD] = _diff_finish(comps[0], comps[1], lam, go_ref, lam_init)


def _diff_attn_sample(q, kc, vc, kn16, vn16, tabs, gq3, lam3, go3, layer, lam_init):
    b, t, w = q.shape
    past = kc.shape[2]
    row = pl.BlockSpec((None, t, w), lambda bi: (bi, 0, 0))
    tspec = pl.BlockSpec((t, LANES), lambda bi: (0, 0))
    in_specs = [row,
                pl.BlockSpec((None, None, past, 2 * A_HEADS, A_HD), lambda bi: (layer, bi, 0, 0, 0)),
                pl.BlockSpec((None, None, past, A_HEADS, A_VD), lambda bi: (layer, bi, 0, 0, 0)),
                row, row, tspec, tspec, tspec,
                pl.BlockSpec((None, 1, A_HD), lambda bi: (layer, 0, 0)),
                pl.BlockSpec((None, 4, A_HD), lambda bi: (layer, 0, 0)),
                pl.BlockSpec((None, 1, A_VD), lambda bi: (layer, 0, 0))]
    est = 4 * _nbytes((past, w), F32) + 16 * _nbytes((past, A_VD), F32)
    return pl.pallas_call(
        functools.partial(_diff_attn_sample_kernel, past=past, lam_init=lam_init),
        grid=(b,),
        in_specs=in_specs,
        out_specs=row,
        out_shape=jax.ShapeDtypeStruct((b, t, w), BF16),
        compiler_params=_params(("parallel",), est),
        name="diff_attn_sample",
    )(q, kc, vc, kn16, vn16, *tabs, gq3, lam3, go3)


def _fox_q(x, gq_ref):
    return (_rms_rows(x.astype(F32), gq_ref[...]) * (F_HD ** -0.5 * LOG2E)).astype(BF16)


def _fox_softmax_pv(pieces, cq, values):
    mt = functools.reduce(jnp.maximum, [jnp.max(s, axis=-1, keepdims=True) for s in pieces])
    shift = (mt + cq) - cq
    ps = [jnp.exp2(s - shift) for s in pieces]
    l = functools.reduce(jnp.add, [jnp.sum(p, axis=-1, keepdims=True) for p in ps])
    o = functools.reduce(jnp.add, [jnp.dot(p.astype(BF16), v, preferred_element_type=F32)
                                   for p, v in zip(ps, values)])
    return o / l


def _fox_attn_prompt_kernel(q_ref, k_ref, v_ref, cq_ref, ck_ref, gq_ref, o_ref, *, bq):
    t = q_ref.shape[0]
    h = pl.program_id(1)
    q = _fox_q(q_ref[...], gq_ref)
    cq = _lane_pick(cq_ref[...], h) * LOG2E
    ck = ck_ref[...] * LOG2E
    r = lax.broadcasted_iota(jnp.int32, (bq, bq), 0)
    j = lax.broadcasted_iota(jnp.int32, (bq, bq), 1)
    mask = j <= r
    for r0 in range(0, t, bq):
        n = r0 + bq
        s = lax.dot_general(q[r0:n], k_ref[:n, :], _NT, preferred_element_type=F32) - ck[:, :n]
        pieces = [jnp.where(mask, s[:, r0:], NEG_INF)]
        values = [v_ref[r0:n, :]]
        if r0:
            pieces.insert(0, s[:, :r0])
            values.insert(0, v_ref[:r0, :])
        o_ref[r0:n, :] = _fox_softmax_pv(pieces, cq[r0:n], values).astype(o_ref.dtype)


def _fox_attn_prompt(q, k16, v16, c_col, c_row, gq3, layer):
    b, t, w = q.shape
    bq = min(ATTN_BLOCK, t)
    hspec = pl.BlockSpec((None, t, F_HD), lambda bi, h: (bi, 0, h))
    in_specs = [hspec, hspec, hspec,
                pl.BlockSpec((None, t, F_HEADS), lambda bi, h: (bi, 0, 0)),
                pl.BlockSpec((None, None, 1, t), lambda bi, h: (bi, h, 0, 0)),
                pl.BlockSpec((None, 1, F_HD), lambda bi, h: (layer, 0, 0))]
    est = 8 * _nbytes((t, F_HD), BF16) + 4 * _nbytes((t, LANES), F32) + 8 * _nbytes((bq, t), F32)
    return pl.pallas_call(
        functools.partial(_fox_attn_prompt_kernel, bq=bq),
        grid=(b, F_HEADS),
        in_specs=in_specs,
        out_specs=hspec,
        out_shape=jax.ShapeDtypeStruct((b, t, w), BF16),
        compiler_params=_params(("parallel", "parallel"), est),
        name="fox_attn_prompt",
    )(q, k16, v16, c_col, c_row, gq3)


def _fox_attn_sample_kernel(q_ref, kc_ref, vc_ref, kn_ref, vn_ref, cq_ref, ck_ref, gq_ref, o_ref, *, past):
    tq = q_ref.shape[0]
    nh = kc_ref.shape[1]
    head0 = pl.program_id(1) * nh
    r = lax.broadcasted_iota(jnp.int32, (tq, tq), 0)
    j = lax.broadcasted_iota(jnp.int32, (tq, tq), 1)
    mask = j <= r
    kc16 = _token_major(kc_ref[...])
    vc16 = _token_major(vc_ref[...])
    for hh in range(nh):
        cols = slice(hh * F_HD, (hh + 1) * F_HD)
        q = _fox_q(q_ref[:, cols], gq_ref)
        cq = _lane_pick(cq_ref[...], head0 + hh) * LOG2E
        ck = ck_ref[hh] * LOG2E
        sc = lax.dot_general(q, kc16[:, cols], _NT, preferred_element_type=F32) - ck[:, :past]
        sn = lax.dot_general(q, kn_ref[:, cols], _NT, preferred_element_type=F32) - ck[:, past:]
        o = _fox_softmax_pv([sc, jnp.where(mask, sn, NEG_INF)], cq, [vc16[:, cols], vn_ref[:, cols]])
        o_ref[:, cols] = o.astype(o_ref.dtype)


def _fox_attn_sample(q, kc, vc, kn16, vn16, c_new, c_row, gq3, layer):
    b, t, w = q.shape
    past = kc.shape[2]
    nh = SUBLANES
    gw = nh * F_HD
    row = pl.BlockSpec((None, t, gw), lambda bi, g: (bi, 0, g))
    cspec = pl.BlockSpec((None, None, past, nh, F_HD), lambda bi, g: (layer, bi, 0, g, 0))
    in_specs = [row, cspec, cspec, row, row,
                pl.BlockSpec((None, t, F_HEADS), lambda bi, g: (bi, 0, 0)),
                pl.BlockSpec((None, nh, 1, past + t), lambda bi, g: (bi, g, 0, 0)),
                pl.BlockSpec((None, 1, F_HD), lambda bi, g: (layer, 0, 0))]
    est = 4 * _nbytes((past, gw), F32) + 16 * _nbytes((past, A_VD), F32)
    return pl.pallas_call(
        functools.partial(_fox_attn_sample_kernel, past=past),
        grid=(b, F_HEADS // nh),
        in_specs=in_specs,
        out_specs=row,
        out_shape=jax.ShapeDtypeStruct((b, t, w), BF16),
        compiler_params=_params(("parallel", "parallel"), est),
        name="fox_attn_sample",
    )(q, kc, vc, kn16, vn16, c_new, c_row, gq3)


def _ssd_steps_kernel(raw_ref, dtb_ref, al_ref, dt_ref, cumc_ref, cumr_ref, *, t, chunk):
    tp = raw_ref.shape[0]
    dt = _softplus(raw_ref[...] + dtb_ref[...])
    if tp > t:
        dt = jnp.where(lax.broadcasted_iota(jnp.int32, dt.shape, 0) < t, dt, 0.0)
    dt_ref[...] = dt
    a = -jnp.exp(al_ref[...])
    for c in range(tp // chunk):
        cum = _prefix_sum(dt[c * chunk:(c + 1) * chunk] * a, 0)
        cumc_ref[c * chunk:(c + 1) * chunk, :] = cum
        cumr_ref[c] = cum.T


def _ssd_steps(small_pad, pr, layer, t, chunk):
    b, tp, w = small_pad.shape
    nc = tp // chunk
    row = pl.BlockSpec((None, tp, w), lambda i: (i, 0, 0))
    par = pl.BlockSpec((None, 1, w), lambda i: (layer, 0, 0))
    est = 16 * _nbytes((tp, w), F32)
    return pl.pallas_call(
        functools.partial(_ssd_steps_kernel, t=t, chunk=chunk),
        grid=(b,),
        in_specs=[row, par, par],
        out_specs=[row, row, pl.BlockSpec((None, nc, w, chunk), lambda i: (i, 0, 0, 0))],
        out_shape=[jax.ShapeDtypeStruct((b, tp, w), F32), jax.ShapeDtypeStruct((b, tp, w), F32),
                   jax.ShapeDtypeStruct((b, nc, w, chunk), F32)],
        compiler_params=_params(("parallel",), est),
        name="ssd_steps",
    )(small_pad, pr["dt_bias_c"], pr["a_log_c"])


def _ssd_kernel(xs_ref, bm_ref, cm_ref, z_ref, dt_ref, cumc_ref, cumr_ref,
                cx_ref, cb_ref, cc_ref, h0_ref,
                wx_ref, wb_ref, wc_ref, bx_ref, bb_ref, bc_ref,
                d_ref, ng_ref,
                o_ref, h1_ref,
                pad_s, xs_s, bm_s, cm_s, h_s, *, t, tp, chunk):
    e_heads = M_GROUP_HEADS
    hd = M_HEAD_DIM
    nc = tp // chunk
    head0 = pl.program_id(1) * e_heads + DT_LANE0

    def conv(raw_ref, hist_ref, w_ref, b_ref, dst):
        wdt = raw_ref.shape[1]
        pad_s[SUBLANES - (M_CONV - 1):SUBLANES, :wdt] = hist_ref[...].astype(F32)
        pad_s[SUBLANES:SUBLANES + t, :wdt] = raw_ref[...].astype(F32)
        rb = min(t, 256)
        for r0 in range(0, t, rb):
            acc = jnp.zeros((rb, wdt), F32) + b_ref[...]
            for k in range(M_CONV):
                lo = r0 + SUBLANES - (M_CONV - 1) + k
                acc = acc + w_ref[k:k + 1, :] * pad_s[lo:lo + rb, :wdt]
            dst[r0:r0 + rb, :] = _silu(acc)
        if tp > t:
            dst[t:tp, :] = jnp.zeros((tp - t, wdt), F32)

    conv(xs_ref, cx_ref, wx_ref, bx_ref, xs_s)
    conv(bm_ref, cb_ref, wb_ref, bb_ref, bm_s)
    conv(cm_ref, cc_ref, wc_ref, bc_ref, cm_s)

    h_s[...] = h0_ref[...]
    causal = (lax.broadcasted_iota(jnp.int32, (chunk, chunk), 1)
              <= lax.broadcasted_iota(jnp.int32, (chunk, chunk), 0))
    lane = lax.broadcasted_iota(jnp.int32, (chunk, M_GROUP_W), 1)

    def spread(cols):
        out = cols[e_heads - 1]
        for e in range(e_heads - 2, -1, -1):
            out = jnp.where(lane[:cols[e].shape[0]] < (e + 1) * hd, cols[e], out)
        return out

    def body(c, carry):
        r0 = c * chunk if isinstance(c, int) else pl.multiple_of(c * chunk, chunk)
        rows = pl.ds(r0, chunk)
        dt_c = dt_ref[rows, :]
        cum_c = cumc_ref[rows, :]
        cum_cols = [_lane_pick(cum_c, head0 + e) for e in range(e_heads)]
        cum_rows = [cumr_ref[c, pl.ds(head0 + e, 1), :] for e in range(e_heads)]
        xs = xs_s[rows, :]
        bm = bm_s[rows, :].astype(BF16)
        cm = cm_s[rows, :].astype(BF16)
        cum_x = spread(cum_cols)
        xdt = xs * spread([_lane_pick(dt_c, head0 + e) for e in range(e_heads)])
        xdt16 = xdt.astype(BF16)
        g = lax.dot_general(cm, bm, _NT, preferred_element_type=F32)
        ys = []
        for e in range(e_heads):
            seg = cum_cols[e] - cum_rows[e]
            mat = g * jnp.exp(jnp.where(causal, seg, -jnp.inf))
            ys.append(jnp.dot(mat.astype(BF16), xdt16[:, e * hd:(e + 1) * hd], preferred_element_type=F32))
        y = jnp.concatenate(ys, axis=1)
        h = h_s[...]
        y = y + jnp.exp(cum_x) * lax.dot_general(cm, h.astype(BF16), _NT, preferred_element_type=F32)
        to_end = jnp.exp(cum_x[chunk - 1:chunk, :] - cum_x)
        st = lax.dot_general((xdt * to_end).astype(BF16), bm, (((0,), (0,)), ((), ())),
                             preferred_element_type=F32)
        for e in range(e_heads):
            dec = jnp.exp(cum_rows[e][:, chunk - 1:chunk])
            h_s[e * hd:(e + 1) * hd, :] = h[e * hd:(e + 1) * hd, :] * dec + st[e * hd:(e + 1) * hd, :]
        d_x = spread([_lane_pick(d_ref[...], head0 + e) for e in range(e_heads)])
        y = y + d_x * xs
        if tp > t:
            y = y[:t] * _silu(z_ref[...].astype(F32))
            o_ref[...] = _rms_rows(y, ng_ref[...]).astype(o_ref.dtype)
        else:
            y = y * _silu(z_ref[rows, :].astype(F32))
            o_ref[rows, :] = _rms_rows(y, ng_ref[...]).astype(o_ref.dtype)
        return carry

    if nc == 1:
        body(0, 0)
    else:
        assert tp == t
        lax.fori_loop(0, nc, body, 0)
    h1_ref[...] = h_s[...]


def _ssd(xbc, z, small, conv0, conv0_layer, ssm0, ssm0_layer, pr, layer):
    b, t, _ = xbc.shape
    chunk = min(SSD_CHUNK, -(-t // LANES) * LANES)
    tp = -(-t // chunk) * chunk
    nc = tp // chunk
    gw = M_GROUP_W
    ns = M_STATE
    b_off = M_INNER // ns
    c_off = (M_INNER + M_GROUPS * ns) // ns
    dt, cumc, cumr = _ssd_steps(jnp.pad(small, ((0, 0), (0, tp - t), (0, 0))), pr, layer, t, chunk)

    def bspec(width, off):
        return pl.BlockSpec((None, t, width), lambda bi, g: (bi, 0, off + g))

    def hspec(width, off):
        return pl.BlockSpec((None, None, M_CONV - 1, width), lambda bi, g: (conv0_layer, bi, 0, off + g))

    def wspec(rows, width, off):
        return pl.BlockSpec((None, rows, width), lambda bi, g: (layer, 0, off + g))

    steps = pl.BlockSpec((None, tp, SMALL_W), lambda bi, g: (bi, 0, 0))
    in_specs = [bspec(gw, 0), bspec(ns, b_off), bspec(ns, c_off), bspec(gw, 0),
                steps, steps,
                pl.BlockSpec((None, nc, SMALL_W, chunk), lambda bi, g: (bi, 0, 0, 0)),
                hspec(gw, 0), hspec(ns, b_off), hspec(ns, c_off),
                pl.BlockSpec((None, None, gw, ns), lambda bi, g: (ssm0_layer, bi, g, 0)),
                wspec(M_CONV, gw, 0), wspec(M_CONV, ns, b_off), wspec(M_CONV, ns, c_off),
                wspec(1, gw, 0), wspec(1, ns, b_off), wspec(1, ns, c_off),
                pl.BlockSpec((None, 1, SMALL_W), lambda bi, g: (layer, 0, 0)),
                wspec(1, gw, 0)]
    out_specs = [bspec(gw, 0), pl.BlockSpec((None, gw, ns), lambda bi, g: (bi, g, 0))]
    out_shape = [jax.ShapeDtypeStruct((b, t, M_INNER), BF16),
                 jax.ShapeDtypeStruct((b, M_HEADS * M_HEAD_DIM, ns), F32)]
    scratch = [pltpu.VMEM((t + SUBLANES, gw), F32), pltpu.VMEM((tp, gw), F32), pltpu.VMEM((tp, ns), F32),
               pltpu.VMEM((tp, ns), F32), pltpu.VMEM((gw, ns), F32)]
    est = 2 * (3 * _nbytes((t, gw), BF16) + 2 * _nbytes((t, ns), BF16) + 3 * _nbytes((tp, SMALL_W), F32))
    est += 2 * _nbytes((tp + SUBLANES, gw), F32) + 2 * _nbytes((tp, ns), F32)
    est += 32 * _nbytes((chunk, max(chunk, gw)), F32)
    return pl.pallas_call(
        functools.partial(_ssd_kernel, t=t, tp=tp, chunk=chunk),
        grid=(b, M_GROUPS),
        in_specs=in_specs,
        out_specs=out_specs,
        out_shape=out_shape,
        scratch_shapes=scratch,
        compiler_params=_params(("parallel", "parallel"), est),
        name="ssd",
    )(xbc, xbc, xbc, z, dt, cumc, cumr, conv0, conv0, conv0, ssm0,
      pr["conv_w"], pr["conv_w"], pr["conv_w"], pr["conv_b"], pr["conv_b"], pr["conv_b"],
      pr["d_c"], pr["m_norm"])


def _rope_tables(pos):
    rot = A_HD // ROPE_FRACTION
    half = rot // 2
    inv_freq = jnp.float32(ROPE_THETA) ** (-jnp.arange(half, dtype=F32) / half)
    ang = pos.astype(F32)[:, None] * inv_freq[None, :]
    cos, sin = jnp.cos(ang), jnp.sin(ang)
    n = pos.shape[0]
    tc = jnp.concatenate([cos, cos, jnp.ones((n, A_HD - rot), F32)], axis=1)
    ta = jnp.concatenate([-sin, jnp.zeros((n, A_HD - half), F32)], axis=1)
    tb = jnp.concatenate([jnp.zeros((n, half), F32), sin, jnp.zeros((n, A_HD - rot), F32)], axis=1)
    return tc, ta, tb


def _lanes(p, lane0):
    depth, n = p.shape
    return jnp.pad(p, ((0, 0), (lane0, SMALL_W - lane0 - n))).reshape(depth, 1, SMALL_W)


def _layer(x, st, past, wts, pr, tabs_seq, layer, depth, conv0_layer, ssm0_layer, kv_prev):
    b, t, d = x.shape
    m = b * t
    a_kc, a_vc, conv0, ssm0, f_kc, f_vc, f_lfc = st
    lam_init = 0.8 - 0.6 * math.exp(-0.3 * layer)
    d_ff = wts["ffn1_out"].shape[1]
    tm = _tile(m, 1024)
    x2 = x.reshape(m, d)

    def ffn(x2, norm_g, w_in, w_out):
        hn = _rmsnorm(x2, norm_g, layer)
        tn = _tile(d_ff, 512)
        (hid,) = _dense("ffn_up", hn, w_in, layer, (0, d_ff), d_ff, tm, tn, _epi_swiglu, (BF16,))
        tmd, tn = _tile(m, 512), _tile(d, 512)
        res = (x2, pl.BlockSpec((tmd, tn), lambda i, j: (i, j)))
        (y,) = _dense("ffn_down", hid, w_out, layer, (0,), d, tmd, tn, _epi_residual_half, (F32,), extras=(res,))
        return y

    x2 = ffn(x2, pr["norm_ffn1"], wts["ffn1_in"], wts["ffn1_out"])

    hn = _rmsnorm(x2, pr["norm_mix"], layer)
    w_seg = wts["w_in"]
    aw = A_HEADS * 2 * A_HD
    fw = F_HEADS * F_HD

    def proj(name, key, width, epi, dtype, extras=()):
        tn = _tile(width, 1024)
        (o,) = _dense(name, hn, w_seg[key], layer, (0,), width, tm, tn, epi, (dtype,), extras=extras)
        return o

    nb, tt = (1, tm) if tm <= t else (tm // t, t)
    nt = t // tt
    ktn = KV_ROWS * KV_W

    def proj_kv(name, key, mode, slot, extras=(), wide_rows=False):
        if wide_rows:
            native = (jax.ShapeDtypeStruct((depth, b, t, KV_ROWS, aw // KV_ROWS), F32),
                      pl.BlockSpec((None, nb, tt, KV_ROWS, KV_W),
                                   lambda i, j: (layer, i // nt, i % nt, 0, j)))
        else:
            native = (jax.ShapeDtypeStruct((depth, b, t, aw // KV_W, KV_W), F32),
                      pl.BlockSpec((None, nb, tt, KV_ROWS, KV_W),
                                   lambda i, j: (layer, i // nt, i % nt, j, 0)))
        copy16 = (jax.ShapeDtypeStruct((m, aw), BF16), pl.BlockSpec((tm, ktn), lambda i, j: (i, j)))
        carried = {} if kv_prev is None else {0: kv_prev[slot]}
        return _dense(name, hn, w_seg[key], layer, (0,), aw, tm, ktn, functools.partial(_epi_kv, mode=mode),
                      (native, copy16), extras=extras, carried=carried)

    if nb == 1:
        tab_specs = [(tb_, pl.BlockSpec((tm, LANES), lambda i, j: (i % nt, 0))) for tb_ in tabs_seq]
    else:
        tab_specs = [(jnp.tile(tb_, (nb, 1)), pl.BlockSpec((tm, LANES), lambda i, j: (0, 0)))
                     for tb_ in tabs_seq]

    def gain_spec(arr):
        return (arr, pl.BlockSpec((None, 1, arr.shape[-1]), lambda i, j: (layer, 0, 0)))

    aq = proj("proj_aq", "aq", aw, _epi_cast, BF16)
    ak_n, ak16 = proj_kv("proj_ak", "ak", "norm_rope", 0, extras=(gain_spec(pr["a_k_norm"]), *tab_specs))
    av_n, av16 = proj_kv("proj_av", "av", "cast", 1, wide_rows=True)
    mz = proj("proj_mz", "mz", M_INNER, _epi_cast, BF16)
    mxbc = proj("proj_mxbc", "mxbc", M_CONV_CH, _epi_cast, BF16)
    fq = proj("proj_fq", "fq", fw, _epi_cast, BF16)
    fk_n, fk16 = proj_kv("proj_fk", "fk", "norm", 2, extras=(gain_spec(pr["f_k_norm"]),))
    fv_n, fv16 = proj_kv("proj_fv", "fv", "cast", 3)
    gtn = _tile(N_BRANCH * d, 1024)
    bias = (pr["b_gate"], pl.BlockSpec((None, 1, gtn), lambda i, j: (layer, 0, j)))
    gates = proj("proj_gate", "gt", N_BRANCH * d, _epi_gate, BF16, extras=(bias,))
    small = proj("proj_small", "small", SMALL_W, _epi_cast, F32).reshape(b, t, SMALL_W)

    a_args = (tabs_seq, pr["a_q_norm"], pr["a_lambda"], pr["a_out_norm"], layer, lam_init)
    aq3, ak3, av3 = aq.reshape(b, t, aw), ak16.reshape(b, t, aw), av16.reshape(b, t, aw)
    if past:
        oa = _diff_attn_sample(aq3, a_kc, a_vc, ak3, av3, *a_args)
    else:
        oa = _diff_attn_prompt(aq3, ak3, av3, *a_args)

    xbc3 = mxbc.reshape(b, t, M_CONV_CH)
    ob, ssm1 = _ssd(xbc3, mz.reshape(b, t, M_INNER), small, conv0, conv0_layer, ssm0, ssm0_layer, pr, layer)
    conv1 = jnp.concatenate([conv0[conv0_layer], xbc3[:, t - min(t, M_CONV - 1):].astype(F32)],
                            axis=1)[:, -(M_CONV - 1):]

    ff_lanes = slice(FF_LANE0, FF_LANE0 + F_HEADS)
    hist = None
    if past:
        hist = jnp.pad(f_lfc[layer], ((0, 0), (0, 0), (FF_LANE0, SMALL_W - FF_LANE0 - F_HEADS)))
    lf_w, c_w = _logf_cumsum(small, pr["f_b"], layer, hist)
    lf = lf_w[:, :, ff_lanes]
    c_all = c_w[:, :, ff_lanes]
    c_new = c_all[:, past:]
    c_row = c_all.transpose(0, 2, 1).reshape(b, F_HEADS, 1, past + t)
    fq3, fk3, fv3 = fq.reshape(b, t, fw), fk16.reshape(b, t, fw), fv16.reshape(b, t, fw)
    if past:
        oc = _fox_attn_sample(fq3, f_kc, f_vc, fk3, fv3, c_new, c_row, pr["f_q_norm"], layer)
    else:
        oc = _fox_attn_prompt(fq3, fk3, fv3, c_new, c_row, pr["f_q_norm"], layer)

    merged = _merge(oa.reshape(m, BRANCH_W), ob.reshape(m, BRANCH_W), oc.reshape(m, BRANCH_W),
                    wts["w_branch"], gates, layer)
    tn = _tile(d, 512)
    res = (x2, pl.BlockSpec((tm, tn), lambda i, j: (i, j)))
    (x2,) = _dense("out_proj", merged, wts["w_out"], layer, (0,), d, tm, tn, _epi_residual, (F32,), extras=(res,))

    x2 = ffn(x2, pr["norm_ffn2"], wts["ffn2_in"], wts["ffn2_out"])

    small_new = (conv1, ssm1.reshape(b, M_HEADS, M_HEAD_DIM, M_STATE), lf)
    return x2.reshape(b, t, d), (ak_n, av_n, fk_n, fv_n), small_new


def _prep_w_in(w_in, d):
    aw = A_HEADS * 2 * A_HD
    fw = F_HEADS * F_HD
    sizes = (aw, aw, A_HEADS * A_VD, M_INNER, M_CONV_CH, M_HEADS, fw, fw, fw, F_HEADS, N_BRANCH * d)
    names = ("aq", "ak", "av", "mz", "mxbc", "mdt", "fq", "fk", "fv", "ff", "gt")
    assert sum(sizes) == w_in.shape[-1]
    start, src = 0, {}
    for n, s in zip(names, sizes):
        src[n] = (start, s)
        start += s
    segs = {}
    for n in ("aq", "ak", "av", "mz", "mxbc", "fq", "fk", "fv", "gt"):
        s0, s = src[n]
        piece = w_in[:, :, s0:s0 + s]
        if n == "av":
            piece = piece.reshape(piece.shape[:2] + (A_HEADS, 2, A_HD)).swapaxes(2, 3).reshape(piece.shape)
        segs[n] = piece.astype(BF16)
    assert DT_LANE0 == 0 and FF_LANE0 == M_HEADS
    segs["small"] = jnp.concatenate(
        [w_in[:, :, src["mdt"][0]:src["mdt"][0] + M_HEADS], w_in[:, :, src["ff"][0]:src["ff"][0] + F_HEADS],
         jnp.zeros(w_in.shape[:2] + (SMALL_W - M_HEADS - F_HEADS,), w_in.dtype)], axis=-1).astype(BF16)
    return segs


def kernel(x_prompt, x_sample, cache_a_k, cache_a_v, state_conv, state_ssm, cache_f_k, cache_f_v, cache_f_logf, norm_ffn1, w_ffn1_in, w_ffn1_out, norm_mix, w_in, b_gate, a_q_norm, a_k_norm, a_lambda, a_out_norm, m_conv_w, m_conv_b, m_dt_bias, m_a_log, m_d, m_norm, f_q_norm, f_k_norm, f_b, w_branch, w_out, norm_ffn2, w_ffn2_in, w_ffn2_out):
    depth = w_in.shape[0]
    bp, tp, d = x_prompt.shape
    bs, ts, _ = x_sample.shape
    past = cache_a_k.shape[2]

    wts = dict(ffn1_in=w_ffn1_in.astype(BF16), ffn1_out=w_ffn1_out.astype(BF16), w_in=_prep_w_in(w_in, d),
               w_branch=w_branch.astype(BF16), w_out=w_out.astype(BF16),
               ffn2_in=w_ffn2_in.astype(BF16), ffn2_out=w_ffn2_out.astype(BF16))

    def row3(p):
        return p.reshape(depth, 1, p.shape[-1])

    pr = dict(norm_ffn1=row3(norm_ffn1), norm_mix=row3(norm_mix), norm_ffn2=row3(norm_ffn2),
              b_gate=b_gate.reshape(depth, 1, N_BRANCH * d),
              a_q_norm=row3(a_q_norm), a_k_norm=row3(a_k_norm), a_lambda=a_lambda, a_out_norm=row3(a_out_norm),
              conv_w=m_conv_w, conv_b=row3(m_conv_b), m_norm=row3(m_norm),
              dt_bias_c=_lanes(m_dt_bias, DT_LANE0), a_log_c=_lanes(m_a_log, DT_LANE0), d_c=_lanes(m_d, DT_LANE0),
              f_q_norm=row3(f_q_norm), f_k_norm=row3(f_k_norm), f_b=_lanes(f_b, FF_LANE0))

    tabs_p = _rope_tables(jnp.arange(tp, dtype=jnp.int32))
    tabs_s = _rope_tables(past + jnp.arange(ts, dtype=jnp.int32))

    st_p = (None, None, jnp.zeros((1, bp, M_CONV - 1, M_CONV_CH), F32),
            jnp.zeros((1, bp, M_HEADS * M_HEAD_DIM, M_STATE), F32), None, None, None)
    st_s = (cache_a_k.reshape(depth, bs, past, 2 * A_HEADS, A_HD), cache_a_v,
            state_conv, state_ssm.reshape(depth, bs, M_HEADS * M_HEAD_DIM, M_STATE),
            cache_f_k, cache_f_v, cache_f_logf)

    y_p, y_s = x_prompt, x_sample
    kv_p = kv_s = None
    new_p, new_s = [], []
    for layer in range(depth):
        y_p, kv_p, n_p = _layer(y_p, st_p, 0, wts, pr, tabs_p, layer, depth, 0, 0, kv_p)
        y_s, kv_s, n_s = _layer(y_s, st_s, past, wts, pr, tabs_s, layer, depth, layer, layer, kv_s)
        new_p.append(n_p)
        new_s.append(n_s)

    def stack(rows, i):
        return jnp.stack([r[i] for r in rows], axis=0)

    def group_out(y, kv, new, b, t):
        ak_n, av_n, fk_n, fv_n = kv
        return (ak_n.reshape(depth, b, t, A_HEADS, 2, A_HD), av_n, stack(new, 0), stack(new, 1),
                fk_n, fv_n, stack(new, 2))

    out_p = group_out(y_p, kv_p, new_p, bp, tp)
    out_s = group_out(y_s, kv_s, new_s, bs, ts)
    return (y_p, y_s) + out_p + out_s
```

```python
import functools
import math

import jax
import jax.numpy as jnp
from jax import lax
from jax.experimental import pallas as pl
from jax.experimental.pallas import tpu as pltpu

F32 = jnp.float32
BF16 = jnp.bfloat16

CHUNK = 64
N_BRANCH = 3
BRANCH_W = 2048
A_HEADS = 8
A_HD = 128
A_VD = 2 * A_HD
M_INNER = 2048
M_HEAD_DIM = 64
M_HEADS = M_INNER // M_HEAD_DIM
M_GROUPS = 8
M_GROUP_HEADS = M_HEADS // M_GROUPS
M_GROUP_W = M_INNER // M_GROUPS
M_STATE = 128
M_CONV = 4
M_CONV_CH = M_INNER + 2 * M_GROUPS * M_STATE
F_HEADS = 16
F_HD = 128
ROPE_THETA = 500000.0
ROPE_FRACTION = 4
ROPE_HALF = A_HD // ROPE_FRACTION // 2
EPS = 1e-6
NEG_INF = -1e30
LOG2E = 1.4426950408889634

LANES = 128
SUBLANES = 8
V7X_VMEM_BYTES = 64 * 1024 * 1024
VMEM_CAP = V7X_VMEM_BYTES - 6 * 1024 * 1024
VMEM_INTERNAL = 4 * 1024 * 1024

ATTN_BLOCK = 256
SSD_CHUNK = 256
SMALL_W = LANES
DT_LANE0 = 0
FF_LANE0 = M_HEADS
KV_ROWS = SUBLANES
KV_W = LANES

def _params(sem, est_bytes):
    return pltpu.CompilerParams(dimension_semantics=sem,
                                vmem_limit_bytes=int(min(VMEM_CAP, est_bytes + VMEM_INTERNAL)))


def _nbytes(shape, dtype):
    return math.prod(shape) * jnp.dtype(dtype).itemsize


def _tile(n, pref, align=0):
    t = pref
    while t > SUBLANES and (n % t or align % t):
        t //= 2
    assert n % t == 0 and align % t == 0
    return t


def _sigmoid(x):
    return 0.5 * (1.0 + jnp.tanh(0.5 * x))


def _silu(x):
    return x * _sigmoid(x)


def _softplus(x):
    return jnp.maximum(x, 0.0) + jnp.log(1.0 + jnp.exp(-jnp.abs(x)))


def _rms_rows(x, gain):
    return x * lax.rsqrt(jnp.mean(x * x, axis=-1, keepdims=True) + EPS) * gain


def _rope(y, tc, ts):
    lane = lax.broadcasted_iota(jnp.int32, y.shape, 1)
    partner = jnp.where(lane < ROPE_HALF, pltpu.roll(y, LANES - ROPE_HALF, 1), pltpu.roll(y, ROPE_HALF, 1))
    return y * tc + partner * ts


def _prefix_sum(x, axis):
    n = x.shape[axis]
    idx = lax.broadcasted_iota(jnp.int32, x.shape, axis)
    d = 1
    while d < n:
        x = x + jnp.where(idx >= d, pltpu.roll(x, d, axis), 0.0)
        d *= 2
    return x


def _lane_pick(blk, lane_index):
    lane = lax.broadcasted_iota(jnp.int32, blk.shape, 1)
    return jnp.sum(jnp.where(lane == lane_index, blk, 0.0), axis=-1, keepdims=True)


def _token_major(x):
    tokens, rows, width = x.shape
    return x.reshape(tokens, rows * width).astype(BF16)


def _cast_sumsq_kernel(x_ref, xb_ref, ss_ref):
    x = x_ref[...]
    xb_ref[...] = x.astype(xb_ref.dtype)
    lane = lax.broadcasted_iota(jnp.int32, ss_ref.shape, 1)
    ss_ref[...] = jnp.where(lane == 0, jnp.sum(x * x, axis=-1, keepdims=True), 0.0)


def _cast_sumsq(x):
    m, d = x.shape
    tm = _tile(m, 256)
    est = 2 * (_nbytes((tm, d), F32) + _nbytes((tm, d), BF16)) + 2 * _nbytes((tm, d), F32)
    return pl.pallas_call(
        _cast_sumsq_kernel,
        grid=(m // tm,),
        in_specs=[pl.BlockSpec((tm, d), lambda i: (i, 0))],
        out_specs=[pl.BlockSpec((tm, d), lambda i: (i, 0)), pl.BlockSpec((tm, LANES), lambda i: (i, 0))],
        out_shape=[jax.ShapeDtypeStruct((m, d), BF16), jax.ShapeDtypeStruct((m, LANES), F32)],
        compiler_params=_params(("parallel",), est),
        name="cast_sumsq",
    )(x)


def _dense_kernel(*refs, n_w, n_ex, n_alias, n_out, epilogue, ss_dim, scale_in_epilogue):
    a_ref = refs[0]
    w_refs = refs[1:1 + n_w]
    ex_refs = refs[1 + n_w:1 + n_w + n_ex]
    o0 = 1 + n_w + n_ex + n_alias
    out_refs = refs[o0:o0 + n_out]
    a = a_ref[...]
    parts = [jnp.dot(a, w[...], preferred_element_type=F32) for w in w_refs]
    if ss_dim:
        ss_ref, ex_refs = ex_refs[0], ex_refs[1:]
        r = lax.rsqrt(jnp.sum(ss_ref[...], axis=-1, keepdims=True) / ss_dim + EPS)
        if scale_in_epilogue:
            epilogue(parts, ex_refs, out_refs, r)
            return
        parts = [p * r for p in parts]
    epilogue(parts, ex_refs, out_refs)


def _dense(name, a, w, layer, col_starts, n_cols, tm, tn, epilogue, outs, extras=(), carried=(), row_ss=None,
           sem=("parallel", "parallel"), scale_in_epilogue=False):
    m, kdim = a.shape
    assert w.shape[1] == kdim and m % tm == 0 and n_cols % tn == 0
    in_specs = [pl.BlockSpec((tm, kdim), lambda i, j: (i, 0))]
    for s in col_starts:
        assert s % tn == 0
        in_specs.append(pl.BlockSpec((None, kdim, tn), lambda i, j, s=s: (layer, 0, s // tn + j)))
    if row_ss is not None:
        extras = ((row_ss, pl.BlockSpec((tm, LANES), lambda i, j: (i, 0))),) + tuple(extras)
    in_specs += [spec for _, spec in extras]
    carried = dict(carried)
    n_in = len(in_specs)
    aliases = {}
    for pos, (oi, arr) in enumerate(sorted(carried.items())):
        in_specs.append(pl.BlockSpec(memory_space=pl.ANY))
        aliases[n_in + pos] = oi
    out_specs, out_shape = [], []
    for o in outs:
        if isinstance(o, tuple):
            out_shape.append(o[0])
            out_specs.append(o[1])
        else:
            out_shape.append(jax.ShapeDtypeStruct((m, n_cols), o))
            out_specs.append(pl.BlockSpec((tm, tn), lambda i, j: (i, j)))
    est = 2 * (_nbytes((tm, kdim), BF16) + len(col_starts) * _nbytes((kdim, tn), BF16))
    est += 2 * sum(_nbytes((tm, tn), s.dtype) for s in out_shape)
    est += 2 * sum(_nbytes([d for d in spec.block_shape if d is not None], arr.dtype) for arr, spec in extras)
    est += (len(col_starts) + 1) * _nbytes((tm, tn), F32)
    kern = functools.partial(_dense_kernel, n_w=len(col_starts), n_ex=len(extras), n_alias=len(carried),
                             n_out=len(outs), epilogue=epilogue, ss_dim=kdim if row_ss is not None else 0,
                             scale_in_epilogue=scale_in_epilogue)
    return pl.pallas_call(
        kern,
        grid=(m // tm, n_cols // tn),
        in_specs=in_specs,
        out_specs=out_specs,
        out_shape=out_shape,
        input_output_aliases=aliases,
        compiler_params=_params(sem, est),
        name=name,
    )(a, *([w] * len(col_starts)), *[arr for arr, _ in extras], *[arr for _, arr in sorted(carried.items())])


def _epi_cast(parts, ex, outs):
    outs[0][...] = parts[0].astype(outs[0].dtype)


def _epi_swiglu(parts, ex, outs):
    gate, up = parts
    outs[0][...] = (_silu(gate) * up).astype(outs[0].dtype)


def _epi_residual(parts, ex, outs, scale):
    x_new = ex[0][...] + scale * parts[0]
    outs[0][...] = x_new
    if len(outs) > 1:
        outs[1][...] = x_new.astype(outs[1].dtype)
        ss_ref = outs[2]
        j = pl.program_id(1)

        @pl.when(j == 0)
        def _():
            ss_ref[...] = jnp.zeros(ss_ref.shape, F32)

        lane = lax.broadcasted_iota(jnp.int32, ss_ref.shape, 1)
        ss_ref[...] += jnp.where(lane == j, jnp.sum(x_new * x_new, axis=-1, keepdims=True), 0.0)


def _epi_gate(parts, ex, outs):
    outs[0][...] = _sigmoid(parts[0] + ex[0][...]).astype(outs[0].dtype)


def _epi_kv(parts, ex, outs, r, mode):
    native, copy16 = outs
    nb, tt, rows, width = native.shape
    acc = parts[0]
    tm = acc.shape[0]
    rb = min(tm, 256) if nb == 1 else min(tm, tt * max(1, 256 // tt))
    for r0 in range(0, tm, rb):
        rs = slice(r0, r0 + rb)
        ys = []
        for c in range(rows):
            y = acc[rs, c * width:(c + 1) * width] * r[rs]
            if mode != "cast":
                y = _rms_rows(y, ex[0][...])
            if mode == "norm_rope":
                y = _rope(y, ex[1][rs, :], ex[2][rs, :])
            ys.append(y)
        blk = jnp.concatenate(ys, axis=1)
        if nb == 1:
            native[0, rs] = blk.reshape(rb, rows, width)
        else:
            native[r0 // tt:(r0 + rb) // tt] = blk.reshape(rb // tt, tt, rows, width)
        copy16[rs, :] = blk.astype(copy16.dtype)


def _merge_kernel(oa_ref, ob_ref, oc_ref, wa_ref, wb_ref, wc_ref, ga_ref, gb_ref, gc_ref, o_ref):
    acc = ga_ref[...].astype(F32) * jnp.dot(oa_ref[...], wa_ref[...], preferred_element_type=F32)
    acc += gb_ref[...].astype(F32) * jnp.dot(ob_ref[...], wb_ref[...], preferred_element_type=F32)
    acc += gc_ref[...].astype(F32) * jnp.dot(oc_ref[...], wc_ref[...], preferred_element_type=F32)
    o_ref[...] = acc.astype(o_ref.dtype)


def _merge(oa, ob, oc, w_branch, gates, layer):
    m, bw = oa.shape
    d = w_branch.shape[-1]
    tm = _tile(m, 1024)
    tn = _tile(d, 512)
    nj = d // tn
    a_spec = pl.BlockSpec((tm, bw), lambda i, j: (i, 0))
    in_specs = [a_spec, a_spec, a_spec]
    in_specs += [pl.BlockSpec((None, None, bw, tn), lambda i, j, b=b: (layer, b, 0, j)) for b in range(N_BRANCH)]
    in_specs += [pl.BlockSpec((tm, tn), lambda i, j, b=b: (i, b * nj + j)) for b in range(N_BRANCH)]
    est = 2 * N_BRANCH * (_nbytes((tm, bw), BF16) + _nbytes((bw, tn), BF16) + _nbytes((tm, tn), BF16))
    est += 2 * _nbytes((tm, tn), BF16) + 3 * _nbytes((tm, tn), F32)
    return pl.pallas_call(
        _merge_kernel,
        grid=(m // tm, nj),
        in_specs=in_specs,
        out_specs=pl.BlockSpec((tm, tn), lambda i, j: (i, j)),
        out_shape=jax.ShapeDtypeStruct((m, d), BF16),
        compiler_params=_params(("parallel", "parallel"), est),
        name="merge",
    )(oa, ob, oc, w_branch, w_branch, w_branch, gates, gates, gates)


def _logf_cumsum_kernel(*refs, past):
    if past:
        raw_ref, fb_ref, hist_ref, lf_ref, c_ref = refs
    else:
        raw_ref, fb_ref, lf_ref, c_ref = refs
    x = raw_ref[...] + fb_ref[...]
    lf = jnp.minimum(x, 0.0) - jnp.log(1.0 + jnp.exp(-jnp.abs(x)))
    lf_ref[...] = lf
    if past:
        lf = jnp.concatenate([hist_ref[...], lf], axis=0)
    c_ref[...] = _prefix_sum(lf, 0)


def _logf_cumsum(raw, f_b3, layer, hist=None):
    b, t, w = raw.shape
    past = 0 if hist is None else hist.shape[1]
    in_specs = [pl.BlockSpec((None, t, w), lambda i: (i, 0, 0)),
                pl.BlockSpec((None, 1, w), lambda i: (layer, 0, 0))]
    args = [raw, f_b3]
    if past:
        in_specs.append(pl.BlockSpec((None, past, w), lambda i: (i, 0, 0)))
        args.append(hist)
    est = 12 * _nbytes((t + past, w), F32)
    return pl.pallas_call(
        functools.partial(_logf_cumsum_kernel, past=past),
        grid=(b,),
        in_specs=in_specs,
        out_specs=[pl.BlockSpec((None, t, w), lambda i: (i, 0, 0)),
                   pl.BlockSpec((None, past + t, w), lambda i: (i, 0, 0))],
        out_shape=[jax.ShapeDtypeStruct((b, t, w), F32), jax.ShapeDtypeStruct((b, past + t, w), F32)],
        compiler_params=_params(("parallel",), est),
        name="logf_cumsum",
    )(*args)


_NT = (((1,), (1,)), ((), ()))


def _softmax_pv(pieces, values):
    m = functools.reduce(jnp.maximum, [jnp.max(s, axis=-1, keepdims=True) for s in pieces])
    ps = [jnp.exp2(s - m) for s in pieces]
    l = functools.reduce(jnp.add, [jnp.sum(p, axis=-1, keepdims=True) for p in ps])
    o = functools.reduce(jnp.add, [jnp.dot(p.astype(BF16), v, preferred_element_type=F32)
                                   for p, v in zip(ps, values)])
    return o / l


def _diff_lambda(lam_ref, lam_init):
    lv = lam_ref[...]
    return (jnp.exp(jnp.sum(lv[0:1] * lv[1:2], axis=-1, keepdims=True))
            - jnp.exp(jnp.sum(lv[2:3] * lv[3:4], axis=-1, keepdims=True)) + lam_init)


def _diff_q(x, gq_ref, tc_ref, ts_ref):
    y = _rope(_rms_rows(x.astype(F32), gq_ref[...]), tc_ref[...], ts_ref[...])
    return (y * (A_HD ** -0.5 * LOG2E)).astype(BF16)


def _diff_finish(o0, o1, lam, go_ref, lam_init):
    return (_rms_rows(o0 - lam * o1, go_ref[...]) * (1.0 - lam_init)).astype(BF16)


def _diff_attn_prompt_kernel(q_ref, k_ref, vlo_ref, vhi_ref, tc_ref, ts_ref, gq_ref, lam_ref, go_ref,
                             o_ref, v_s, *, bq, lam_init):
    t = q_ref.shape[0]
    v_s[:, :A_HD] = vlo_ref[...]
    v_s[:, A_HD:] = vhi_ref[...]
    qs = [_diff_q(q_ref[:, c * A_HD:(c + 1) * A_HD], gq_ref, tc_ref, ts_ref) for c in range(2)]
    lam = _diff_lambda(lam_ref, lam_init)
    r = lax.broadcasted_iota(jnp.int32, (bq, bq), 0)
    j = lax.broadcasted_iota(jnp.int32, (bq, bq), 1)
    mask = j // CHUNK <= r // CHUNK
    for r0 in range(0, t, bq):
        n = r0 + bq
        comps = []
        for c in range(2):
            s = lax.dot_general(qs[c][r0:n], k_ref[:n, c * A_HD:(c + 1) * A_HD], _NT, preferred_element_type=F32)
            pieces = [jnp.where(mask, s[:, r0:], NEG_INF)]
            values = [v_s[r0:n, :]]
            if r0:
                pieces.insert(0, s[:, :r0])
                values.insert(0, v_s[:r0, :])
            comps.append(_softmax_pv(pieces, values))
        o_ref[r0:n, :] = _diff_finish(comps[0], comps[1], lam, go_ref, lam_init)


def _diff_attn_prompt(q, k16, v16, tabs, gq3, lam3, go3, layer, lam_init):
    b, t, w = q.shape
    hw = 2 * A_HD
    bq = min(ATTN_BLOCK, t)
    hspec = pl.BlockSpec((None, t, hw), lambda bi, h: (bi, 0, h))
    tspec = pl.BlockSpec((t, LANES), lambda bi, h: (0, 0))
    in_specs = [hspec, hspec,
                pl.BlockSpec((None, t, A_HD), lambda bi, h: (bi, 0, h)),
                pl.BlockSpec((None, t, A_HD), lambda bi, h: (bi, 0, A_HEADS + h)),
                tspec, tspec,
                pl.BlockSpec((None, 1, A_HD), lambda bi, h: (layer, 0, 0)),
                pl.BlockSpec((None, 4, A_HD), lambda bi, h: (layer, 0, 0)),
                pl.BlockSpec((None, 1, A_VD), lambda bi, h: (layer, 0, 0))]
    est = 10 * _nbytes((t, hw), BF16) + 6 * _nbytes((t, LANES), F32) + 8 * _nbytes((bq, t), F32)
    return pl.pallas_call(
        functools.partial(_diff_attn_prompt_kernel, bq=bq, lam_init=lam_init),
        grid=(b, A_HEADS),
        in_specs=in_specs,
        out_specs=hspec,
        out_shape=jax.ShapeDtypeStruct((b, t, w), BF16),
        scratch_shapes=[pltpu.VMEM((t, hw), BF16)],
        compiler_params=_params(("parallel", "parallel"), est),
        name="diff_attn_prompt",
    )(q, k16, v16, v16, *tabs, gq3, lam3, go3)


def _diff_attn_sample_kernel(q_ref, kc_ref, vc_ref, kn_ref, vn_ref, tc_ref, ts_ref, gq_ref, lam_ref,
                             go_ref, o_ref, *, past, lam_init):
    tq = q_ref.shape[0]
    lam = _diff_lambda(lam_ref, lam_init)
    qchunk = (past + lax.broadcasted_iota(jnp.int32, (tq, past), 0)) // CHUNK
    mask_c = lax.broadcasted_iota(jnp.int32, (tq, past), 1) // CHUNK <= qchunk
    qchunk = (past + lax.broadcasted_iota(jnp.int32, (tq, tq), 0)) // CHUNK
    mask_n = (past + lax.broadcasted_iota(jnp.int32, (tq, tq), 1)) // CHUNK <= qchunk
    half = A_HEADS * A_HD
    kc16 = [_token_major(kc_ref[:, g * SUBLANES:(g + 1) * SUBLANES, :]) for g in range(2 * A_HEADS // SUBLANES)]
    vc16 = [_token_major(vc_ref[:, :, s * A_HD:(s + 1) * A_HD]) for s in range(2)]
    for h in range(A_HEADS):
        vc = jnp.concatenate([v[:, h * A_HD:(h + 1) * A_HD] for v in vc16], axis=1)
        vn = jnp.concatenate([vn_ref[:, h * A_HD:(h + 1) * A_HD],
                              vn_ref[:, half + h * A_HD:half + (h + 1) * A_HD]], axis=1)
        comps = []
        for c in range(2):
            col = (2 * h + c) * A_HD
            q = _diff_q(q_ref[:, col:col + A_HD], gq_ref, tc_ref, ts_ref)
            krow = 2 * h + c
            kc = kc16[krow // SUBLANES][:, (krow % SUBLANES) * A_HD:(krow % SUBLANES + 1) * A_HD]
            sc = lax.dot_general(q, kc, _NT, preferred_element_type=F32)
            sn = lax.dot_general(q, kn_ref[:, col:col + A_HD], _NT, preferred_element_type=F32)
            comps.append(_softmax_pv([jnp.where(mask_c, sc, NEG_INF), jnp.where(mask_n, sn, NEG_INF)], [vc, vn]))
        o_ref[:, h * A_VD:(h + 1) * A_VD] = _diff_finish(comps[0], comps[1], lam, go_ref, lam_init)


def _diff_attn_sample(q, kc, vc, kn16, vn16, tabs, gq3, lam3, go3, layer, lam_init):
    b, t, w = q.shape
    past = kc.shape[2]
    row = pl.BlockSpec((None, t, w), lambda bi: (bi, 0, 0))
    tspec = pl.BlockSpec((t, LANES), lambda bi: (0, 0))
    in_specs = [row,
                pl.BlockSpec((None, None, past, 2 * A_HEADS, A_HD), lambda bi: (layer, bi, 0, 0, 0)),
                pl.BlockSpec((None, None, past, A_HEADS, A_VD), lambda bi: (layer, bi, 0, 0, 0)),
                row, row, tspec, tspec,
                pl.BlockSpec((None, 1, A_HD), lambda bi: (layer, 0, 0)),
                pl.BlockSpec((None, 4, A_HD), lambda bi: (layer, 0, 0)),
                pl.BlockSpec((None, 1, A_VD), lambda bi: (layer, 0, 0))]
    est = 4 * _nbytes((past, w), F32) + 16 * _nbytes((past, A_VD), F32)
    return pl.pallas_call(
        functools.partial(_diff_attn_sample_kernel, past=past, lam_init=lam_init),
        grid=(b,),
        in_specs=in_specs,
        out_specs=row,
        out_shape=jax.ShapeDtypeStruct((b, t, w), BF16),
        compiler_params=_params(("parallel",), est),
        name="diff_attn_sample",
    )(q, kc, vc, kn16, vn16, *tabs, gq3, lam3, go3)


def _fox_q(x, gq_ref):
    return (_rms_rows(x.astype(F32), gq_ref[...]) * (F_HD ** -0.5 * LOG2E)).astype(BF16)


def _fox_softmax_pv(pieces, cq, values):
    mt = functools.reduce(jnp.maximum, [jnp.max(s, axis=-1, keepdims=True) for s in pieces])
    shift = (mt + cq) - cq
    ps = [jnp.exp2(s - shift) for s in pieces]
    l = functools.reduce(jnp.add, [jnp.sum(p, axis=-1, keepdims=True) for p in ps])
    o = functools.reduce(jnp.add, [jnp.dot(p.astype(BF16), v, preferred_element_type=F32)
                                   for p, v in zip(ps, values)])
    return o / l


def _fox_attn_prompt_kernel(q_ref, k_ref, v_ref, cq_ref, ck_ref, gq_ref, o_ref, *, bq):
    t = q_ref.shape[0]
    h = pl.program_id(1)
    q = _fox_q(q_ref[...], gq_ref)
    cq = _lane_pick(cq_ref[...], h) * LOG2E
    ck = ck_ref[...] * LOG2E
    r = lax.broadcasted_iota(jnp.int32, (bq, bq), 0)
    j = lax.broadcasted_iota(jnp.int32, (bq, bq), 1)
    mask = j <= r
    for r0 in range(0, t, bq):
        n = r0 + bq
        s = lax.dot_general(q[r0:n], k_ref[:n, :], _NT, preferred_element_type=F32) - ck[:, :n]
        pieces = [jnp.where(mask, s[:, r0:], NEG_INF)]
        values = [v_ref[r0:n, :]]
        if r0:
            pieces.insert(0, s[:, :r0])
            values.insert(0, v_ref[:r0, :])
        o_ref[r0:n, :] = _fox_softmax_pv(pieces, cq[r0:n], values).astype(o_ref.dtype)


def _fox_attn_prompt(q, k16, v16, c_col, c_row, gq3, layer):
    b, t, w = q.shape
    bq = min(ATTN_BLOCK, t)
    hspec = pl.BlockSpec((None, t, F_HD), lambda bi, h: (bi, 0, h))
    in_specs = [hspec, hspec, hspec,
                pl.BlockSpec((None, t, F_HEADS), lambda bi, h: (bi, 0, 0)),
                pl.BlockSpec((None, None, 1, t), lambda bi, h: (bi, h, 0, 0)),
                pl.BlockSpec((None, 1, F_HD), lambda bi, h: (layer, 0, 0))]
    est = 8 * _nbytes((t, F_HD), BF16) + 4 * _nbytes((t, LANES), F32) + 8 * _nbytes((bq, t), F32)
    return pl.pallas_call(
        functools.partial(_fox_attn_prompt_kernel, bq=bq),
        grid=(b, F_HEADS),
        in_specs=in_specs,
        out_specs=hspec,
        out_shape=jax.ShapeDtypeStruct((b, t, w), BF16),
        compiler_params=_params(("parallel", "parallel"), est),
        name="fox_attn_prompt",
    )(q, k16, v16, c_col, c_row, gq3)


def _fox_attn_sample_kernel(q_ref, kc_ref, vc_ref, kn_ref, vn_ref, cq_ref, ck_ref, gq_ref, o_ref, *, past):
    tq = q_ref.shape[0]
    nh = kc_ref.shape[1]
    head0 = pl.program_id(1) * nh
    r = lax.broadcasted_iota(jnp.int32, (tq, tq), 0)
    j = lax.broadcasted_iota(jnp.int32, (tq, tq), 1)
    mask = j <= r
    kc16 = _token_major(kc_ref[...])
    vc16 = _token_major(vc_ref[...])
    for hh in range(nh):
        cols = slice(hh * F_HD, (hh + 1) * F_HD)
        q = _fox_q(q_ref[:, cols], gq_ref)
        cq = _lane_pick(cq_ref[...], head0 + hh) * LOG2E
        ck = ck_ref[hh] * LOG2E
        sc = lax.dot_general(q, kc16[:, cols], _NT, preferred_element_type=F32) - ck[:, :past]
        sn = lax.dot_general(q, kn_ref[:, cols], _NT, preferred_element_type=F32) - ck[:, past:]
        o = _fox_softmax_pv([sc, jnp.where(mask, sn, NEG_INF)], cq, [vc16[:, cols], vn_ref[:, cols]])
        o_ref[:, cols] = o.astype(o_ref.dtype)


def _fox_attn_sample(q, kc, vc, kn16, vn16, c_new, c_row, gq3, layer):
    b, t, w = q.shape
    past = kc.shape[2]
    nh = SUBLANES
    gw = nh * F_HD
    row = pl.BlockSpec((None, t, gw), lambda bi, g: (bi, 0, g))
    cspec = pl.BlockSpec((None, None, past, nh, F_HD), lambda bi, g: (layer, bi, 0, g, 0))
    in_specs = [row, cspec, cspec, row, row,
                pl.BlockSpec((None, t, F_HEADS), lambda bi, g: (bi, 0, 0)),
                pl.BlockSpec((None, nh, 1, past + t), lambda bi, g: (bi, g, 0, 0)),
                pl.BlockSpec((None, 1, F_HD), lambda bi, g: (layer, 0, 0))]
    est = 4 * _nbytes((past, gw), F32) + 16 * _nbytes((past, A_VD), F32)
    return pl.pallas_call(
        functools.partial(_fox_attn_sample_kernel, past=past),
        grid=(b, F_HEADS // nh),
        in_specs=in_specs,
        out_specs=row,
        out_shape=jax.ShapeDtypeStruct((b, t, w), BF16),
        compiler_params=_params(("parallel", "parallel"), est),
        name="fox_attn_sample",
    )(q, kc, vc, kn16, vn16, c_new, c_row, gq3)


def _ssd_steps_kernel(raw_ref, dtb_ref, al_ref, dt_ref, cumc_ref, cumr_ref, *, t, chunk):
    tp = raw_ref.shape[0]
    dt = _softplus(raw_ref[...] + dtb_ref[...])
    if tp > t:
        dt = jnp.where(lax.broadcasted_iota(jnp.int32, dt.shape, 0) < t, dt, 0.0)
    dt_ref[...] = dt
    a = -jnp.exp(al_ref[...])
    for c in range(tp // chunk):
        cum = _prefix_sum(dt[c * chunk:(c + 1) * chunk] * a, 0)
        cumc_ref[c * chunk:(c + 1) * chunk, :] = cum
        cumr_ref[c] = cum.T


def _ssd_steps(small_pad, pr, layer, t, chunk):
    b, tp, w = small_pad.shape
    nc = tp // chunk
    row = pl.BlockSpec((None, tp, w), lambda i: (i, 0, 0))
    par = pl.BlockSpec((None, 1, w), lambda i: (layer, 0, 0))
    est = 16 * _nbytes((tp, w), F32)
    return pl.pallas_call(
        functools.partial(_ssd_steps_kernel, t=t, chunk=chunk),
        grid=(b,),
        in_specs=[row, par, par],
        out_specs=[row, row, pl.BlockSpec((None, nc, w, chunk), lambda i: (i, 0, 0, 0))],
        out_shape=[jax.ShapeDtypeStruct((b, tp, w), F32), jax.ShapeDtypeStruct((b, tp, w), F32),
                   jax.ShapeDtypeStruct((b, nc, w, chunk), F32)],
        compiler_params=_params(("parallel",), est),
        name="ssd_steps",
    )(small_pad, pr["dt_bias_c"], pr["a_log_c"])


def _ssd_kernel(xs_ref, bm_ref, cm_ref, z_ref, dt_ref, cumc_ref, cumr_ref,
                cx_ref, cb_ref, cc_ref, h0_ref,
                wx_ref, wb_ref, wc_ref, bx_ref, bb_ref, bc_ref,
                d_ref, ng_ref,
                o_ref, h1_ref,
                pad_s, xs_s, bm_s, cm_s, h_s, *, t, tp, chunk):
    e_heads = M_GROUP_HEADS
    hd = M_HEAD_DIM
    nc = tp // chunk
    head0 = pl.program_id(1) * e_heads + DT_LANE0

    def conv(raw_ref, hist_ref, w_ref, b_ref, dst):
        wdt = raw_ref.shape[1]
        pad_s[SUBLANES - (M_CONV - 1):SUBLANES, :wdt] = hist_ref[...].astype(F32)
        pad_s[SUBLANES:SUBLANES + t, :wdt] = raw_ref[...].astype(F32)
        rb = min(t, 256)
        for r0 in range(0, t, rb):
            acc = jnp.zeros((rb, wdt), F32) + b_ref[...]
            for k in range(M_CONV):
                lo = r0 + SUBLANES - (M_CONV - 1) + k
                acc = acc + w_ref[k:k + 1, :] * pad_s[lo:lo + rb, :wdt]
            dst[r0:r0 + rb, :] = _silu(acc)
        if tp > t:
            dst[t:tp, :] = jnp.zeros((tp - t, wdt), F32)

    conv(xs_ref, cx_ref, wx_ref, bx_ref, xs_s)
    conv(bm_ref, cb_ref, wb_ref, bb_ref, bm_s)
    conv(cm_ref, cc_ref, wc_ref, bc_ref, cm_s)

    h_s[...] = h0_ref[...]
    causal = (lax.broadcasted_iota(jnp.int32, (chunk, chunk), 1)
              <= lax.broadcasted_iota(jnp.int32, (chunk, chunk), 0))
    lane = lax.broadcasted_iota(jnp.int32, (chunk, M_GROUP_W), 1)

    def spread(cols):
        out = cols[e_heads - 1]
        for e in range(e_heads - 2, -1, -1):
            out = jnp.where(lane[:cols[e].shape[0]] < (e + 1) * hd, cols[e], out)
        return out

    def body(c, carry):
        r0 = c * chunk if isinstance(c, int) else pl.multiple_of(c * chunk, chunk)
        rows = pl.ds(r0, chunk)
        dt_c = dt_ref[rows, :]
        cum_c = cumc_ref[rows, :]
        cum_cols = [_lane_pick(cum_c, head0 + e) for e in range(e_heads)]
        cum_rows = [cumr_ref[c, pl.ds(head0 + e, 1), :] for e in range(e_heads)]
        xs = xs_s[rows, :]
        bm = bm_s[rows, :].astype(BF16)
        cm = cm_s[rows, :].astype(BF16)
        cum_x = spread(cum_cols)
        xdt = xs * spread([_lane_pick(dt_c, head0 + e) for e in range(e_heads)])
        xdt16 = xdt.astype(BF16)
        g = lax.dot_general(cm, bm, _NT, preferred_element_type=F32)
        ys = []
        for e in range(e_heads):
            seg = cum_cols[e] - cum_rows[e]
            mat = g * jnp.exp(jnp.where(causal, seg, -jnp.inf))
            ys.append(jnp.dot(mat.astype(BF16), xdt16[:, e * hd:(e + 1) * hd], preferred_element_type=F32))
        y = jnp.concatenate(ys, axis=1)
        h = h_s[...]
        y = y + jnp.exp(cum_x) * lax.dot_general(cm, h.astype(BF16), _NT, preferred_element_type=F32)
        to_end = jnp.exp(cum_x[chunk - 1:chunk, :] - cum_x)
        st = lax.dot_general((xdt * to_end).astype(BF16), bm, (((0,), (0,)), ((), ())),
                             preferred_element_type=F32)
        for e in range(e_heads):
            dec = jnp.exp(cum_rows[e][:, chunk - 1:chunk])
            h_s[e * hd:(e + 1) * hd, :] = h[e * hd:(e + 1) * hd, :] * dec + st[e * hd:(e + 1) * hd, :]
        d_x = spread([_lane_pick(d_ref[...], head0 + e) for e in range(e_heads)])
        y = y + d_x * xs
        if tp > t:
            y = y[:t] * _silu(z_ref[...].astype(F32))
            o_ref[...] = _rms_rows(y, ng_ref[...]).astype(o_ref.dtype)
        else:
            y = y * _silu(z_ref[rows, :].astype(F32))
            o_ref[rows, :] = _rms_rows(y, ng_ref[...]).astype(o_ref.dtype)
        return carry

    if nc == 1:
        body(0, 0)
    else:
        assert tp == t
        lax.fori_loop(0, nc, body, 0)
    h1_ref[...] = h_s[...]


def _ssd(xbc, z, small, conv0, conv0_layer, ssm0, ssm0_layer, pr, layer):
    b, t, _ = xbc.shape
    chunk = min(SSD_CHUNK, -(-t // LANES) * LANES)
    tp = -(-t // chunk) * chunk
    nc = tp // chunk
    gw = M_GROUP_W
    ns = M_STATE
    b_off = M_INNER // ns
    c_off = (M_INNER + M_GROUPS * ns) // ns
    dt, cumc, cumr = _ssd_steps(jnp.pad(small, ((0, 0), (0, tp - t), (0, 0))), pr, layer, t, chunk)

    def bspec(width, off):
        return pl.BlockSpec((None, t, width), lambda bi, g: (bi, 0, off + g))

    def hspec(width, off):
        return pl.BlockSpec((None, None, M_CONV - 1, width), lambda bi, g: (conv0_layer, bi, 0, off + g))

    def wspec(rows, width, off):
        return pl.BlockSpec((None, rows, width), lambda bi, g: (layer, 0, off + g))

    steps = pl.BlockSpec((None, tp, SMALL_W), lambda bi, g: (bi, 0, 0))
    in_specs = [bspec(gw, 0), bspec(ns, b_off), bspec(ns, c_off), bspec(gw, 0),
                steps, steps,
                pl.BlockSpec((None, nc, SMALL_W, chunk), lambda bi, g: (bi, 0, 0, 0)),
                hspec(gw, 0), hspec(ns, b_off), hspec(ns, c_off),
                pl.BlockSpec((None, None, gw, ns), lambda bi, g: (ssm0_layer, bi, g, 0)),
                wspec(M_CONV, gw, 0), wspec(M_CONV, ns, b_off), wspec(M_CONV, ns, c_off),
                wspec(1, gw, 0), wspec(1, ns, b_off), wspec(1, ns, c_off),
                pl.BlockSpec((None, 1, SMALL_W), lambda bi, g: (layer, 0, 0)),
                wspec(1, gw, 0)]
    out_specs = [bspec(gw, 0), pl.BlockSpec((None, gw, ns), lambda bi, g: (bi, g, 0))]
    out_shape = [jax.ShapeDtypeStruct((b, t, M_INNER), BF16),
                 jax.ShapeDtypeStruct((b, M_HEADS * M_HEAD_DIM, ns), F32)]
    scratch = [pltpu.VMEM((t + SUBLANES, gw), F32), pltpu.VMEM((tp, gw), F32), pltpu.VMEM((tp, ns), F32),
               pltpu.VMEM((tp, ns), F32), pltpu.VMEM((gw, ns), F32)]
    est = 2 * (3 * _nbytes((t, gw), BF16) + 2 * _nbytes((t, ns), BF16) + 3 * _nbytes((tp, SMALL_W), F32))
    est += 2 * _nbytes((tp + SUBLANES, gw), F32) + 2 * _nbytes((tp, ns), F32)
    est += 32 * _nbytes((chunk, max(chunk, gw)), F32)
    return pl.pallas_call(
        functools.partial(_ssd_kernel, t=t, tp=tp, chunk=chunk),
        grid=(b, M_GROUPS),
        in_specs=in_specs,
        out_specs=out_specs,
        out_shape=out_shape,
        scratch_shapes=scratch,
        compiler_params=_params(("parallel", "parallel"), est),
        name="ssd",
    )(xbc, xbc, xbc, z, dt, cumc, cumr, conv0, conv0, conv0, ssm0,
      pr["conv_w"], pr["conv_w"], pr["conv_w"], pr["conv_b"], pr["conv_b"], pr["conv_b"],
      pr["d_c"], pr["m_norm"])


def _rope_tables(pos):
    rot = A_HD // ROPE_FRACTION
    half = rot // 2
    inv_freq = jnp.float32(ROPE_THETA) ** (-jnp.arange(half, dtype=F32) / half)
    ang = pos.astype(F32)[:, None] * inv_freq[None, :]
    cos, sin = jnp.cos(ang), jnp.sin(ang)
    n = pos.shape[0]
    tc = jnp.concatenate([cos, cos, jnp.ones((n, A_HD - rot), F32)], axis=1)
    ts = jnp.concatenate([-sin, sin, jnp.zeros((n, A_HD - rot), F32)], axis=1)
    return tc, ts


def _lanes(p, lane0):
    depth, n = p.shape
    return jnp.pad(p, ((0, 0), (lane0, SMALL_W - lane0 - n))).reshape(depth, 1, SMALL_W)


def _layer(xs, b, t, st, past, wts, pr, tabs_seq, layer, depth, conv0_layer, ssm0_layer, kv_prev, last):
    m, d = xs[0].shape
    a_kc, a_vc, conv0, ssm0, f_kc, f_vc, f_lfc = st
    lam_init = 0.8 - 0.6 * math.exp(-0.3 * layer)
    d_ff = wts["ffn1_out"].shape[1]
    tm = _tile(m, 1024)

    def residual(name, a, w, x2, tmr, scale, emit):
        tn = _tile(d, 512)
        assert d // tn <= LANES
        res = (x2, pl.BlockSpec((tmr, tn), lambda i, j: (i, j)))
        outs = (F32,)
        if emit:
            outs += (BF16, (jax.ShapeDtypeStruct((m, LANES), F32), pl.BlockSpec((tmr, LANES), lambda i, j: (i, 0))))
        got = _dense(name, a, w, layer, (0,), d, tmr, tn, functools.partial(_epi_residual, scale=scale), outs,
                     extras=(res,), sem=("parallel", "arbitrary" if emit else "parallel"))
        return tuple(got) if emit else (got[0], None, None)

    def ffn(xs, w_in, w_out, emit):
        x2, xb, ss = xs
        tn = _tile(d_ff, 512)
        (hid,) = _dense("ffn_up", xb, w_in, layer, (0, d_ff), d_ff, tm, tn, _epi_swiglu, (BF16,), row_ss=ss)
        return residual("ffn_down", hid, w_out, x2, _tile(m, 512), 0.5, emit)

    xs = ffn(xs, wts["ffn1_in"], wts["ffn1_out"], True)
    x2, hn, hn_ss = xs

    w_seg = wts["w_in"]
    aw = A_HEADS * 2 * A_HD
    fw = F_HEADS * F_HD

    def proj(name, key, width, epi, dtype, extras=()):
        tn = _tile(width, 1024)
        (o,) = _dense(name, hn, w_seg[key], layer, (0,), width, tm, tn, epi, (dtype,), extras=extras, row_ss=hn_ss)
        return o

    nb, tt = (1, tm) if tm <= t else (tm // t, t)
    nt = t // tt
    ktn = KV_ROWS * KV_W

    def proj_kv(name, key, mode, slot, extras=(), wide_rows=False):
        if wide_rows:
            native = (jax.ShapeDtypeStruct((depth, b, t, KV_ROWS, aw // KV_ROWS), F32),
                      pl.BlockSpec((None, nb, tt, KV_ROWS, KV_W),
                                   lambda i, j: (layer, i // nt, i % nt, 0, j)))
        else:
            native = (jax.ShapeDtypeStruct((depth, b, t, aw // KV_W, KV_W), F32),
                      pl.BlockSpec((None, nb, tt, KV_ROWS, KV_W),
                                   lambda i, j: (layer, i // nt, i % nt, j, 0)))
        copy16 = (jax.ShapeDtypeStruct((m, aw), BF16), pl.BlockSpec((tm, ktn), lambda i, j: (i, j)))
        carried = {} if kv_prev is None else {0: kv_prev[slot]}
        return _dense(name, hn, w_seg[key], layer, (0,), aw, tm, ktn, functools.partial(_epi_kv, mode=mode),
                      (native, copy16), extras=extras, carried=carried, row_ss=hn_ss, scale_in_epilogue=True)

    if nb == 1:
        tab_specs = [(tb_, pl.BlockSpec((tm, LANES), lambda i, j: (i % nt, 0))) for tb_ in tabs_seq]
    else:
        tab_specs = [(jnp.tile(tb_, (nb, 1)), pl.BlockSpec((tm, LANES), lambda i, j: (0, 0)))
                     for tb_ in tabs_seq]

    def gain_spec(arr):
        return (arr, pl.BlockSpec((None, 1, arr.shape[-1]), lambda i, j: (layer, 0, 0)))

    aq = proj("proj_aq", "aq", aw, _epi_cast, BF16)
    ak_n, ak16 = proj_kv("proj_ak", "ak", "norm_rope", 0, extras=(gain_spec(pr["a_k_norm"]), *tab_specs))
    av_n, av16 = proj_kv("proj_av", "av", "cast", 1, wide_rows=True)
    mz = proj("proj_mz", "mz", M_INNER, _epi_cast, BF16)
    mxbc = proj("proj_mxbc", "mxbc", M_CONV_CH, _epi_cast, BF16)
    fq = proj("proj_fq", "fq", fw, _epi_cast, BF16)
    fk_n, fk16 = proj_kv("proj_fk", "fk", "norm", 2, extras=(gain_spec(pr["f_k_norm"]),))
    fv_n, fv16 = proj_kv("proj_fv", "fv", "cast", 3)
    gtn = _tile(N_BRANCH * d, 1024)
    bias = (pr["b_gate"], pl.BlockSpec((None, 1, gtn), lambda i, j: (layer, 0, j)))
    gates = proj("proj_gate", "gt", N_BRANCH * d, _epi_gate, BF16, extras=(bias,))
    small = proj("proj_small", "small", SMALL_W, _epi_cast, F32).reshape(b, t, SMALL_W)

    a_args = (tabs_seq, pr["a_q_norm"], pr["a_lambda"], pr["a_out_norm"], layer, lam_init)
    aq3, ak3, av3 = aq.reshape(b, t, aw), ak16.reshape(b, t, aw), av16.reshape(b, t, aw)
    if past:
        oa = _diff_attn_sample(aq3, a_kc, a_vc, ak3, av3, *a_args)
    else:
        oa = _diff_attn_prompt(aq3, ak3, av3, *a_args)

    xbc3 = mxbc.reshape(b, t, M_CONV_CH)
    ob, ssm1 = _ssd(xbc3, mz.reshape(b, t, M_INNER), small, conv0, conv0_layer, ssm0, ssm0_layer, pr, layer)
    conv1 = jnp.concatenate([conv0[conv0_layer], xbc3[:, t - min(t, M_CONV - 1):].astype(F32)],
                            axis=1)[:, -(M_CONV - 1):]

    ff_lanes = slice(FF_LANE0, FF_LANE0 + F_HEADS)
    hist = None
    if past:
        hist = jnp.pad(f_lfc[layer], ((0, 0), (0, 0), (FF_LANE0, SMALL_W - FF_LANE0 - F_HEADS)))
    lf_w, c_w = _logf_cumsum(small, pr["f_b"], layer, hist)
    lf = lf_w[:, :, ff_lanes]
    c_all = c_w[:, :, ff_lanes]
    c_new = c_all[:, past:]
    c_row = c_all.transpose(0, 2, 1).reshape(b, F_HEADS, 1, past + t)
    fq3, fk3, fv3 = fq.reshape(b, t, fw), fk16.reshape(b, t, fw), fv16.reshape(b, t, fw)
    if past:
        oc = _fox_attn_sample(fq3, f_kc, f_vc, fk3, fv3, c_new, c_row, pr["f_q_norm"], layer)
    else:
        oc = _fox_attn_prompt(fq3, fk3, fv3, c_new, c_row, pr["f_q_norm"], layer)

    merged = _merge(oa.reshape(m, BRANCH_W), ob.reshape(m, BRANCH_W), oc.reshape(m, BRANCH_W),
                    wts["w_branch"], gates, layer)
    xs = residual("out_proj", merged, wts["w_out"], x2, tm, 1.0, True)
    xs = ffn(xs, wts["ffn2_in"], wts["ffn2_out"], not last)

    small_new = (conv1, ssm1.reshape(b, M_HEADS, M_HEAD_DIM, M_STATE), lf)
    return xs, (ak_n, av_n, fk_n, fv_n), small_new


def _prep_w_in(w_in, d):
    aw = A_HEADS * 2 * A_HD
    fw = F_HEADS * F_HD
    sizes = (aw, aw, A_HEADS * A_VD, M_INNER, M_CONV_CH, M_HEADS, fw, fw, fw, F_HEADS, N_BRANCH * d)
    names = ("aq", "ak", "av", "mz", "mxbc", "mdt", "fq", "fk", "fv", "ff", "gt")
    assert sum(sizes) == w_in.shape[-1]
    start, src = 0, {}
    for n, s in zip(names, sizes):
        src[n] = (start, s)
        start += s
    segs = {}
    for n in ("aq", "ak", "av", "mz", "mxbc", "fq", "fk", "fv", "gt"):
        s0, s = src[n]
        piece = w_in[:, :, s0:s0 + s]
        if n == "av":
            piece = piece.reshape(piece.shape[:2] + (A_HEADS, 2, A_HD)).swapaxes(2, 3).reshape(piece.shape)
        segs[n] = piece.astype(BF16)
    assert DT_LANE0 == 0 and FF_LANE0 == M_HEADS
    segs["small"] = jnp.concatenate(
        [w_in[:, :, src["mdt"][0]:src["mdt"][0] + M_HEADS], w_in[:, :, src["ff"][0]:src["ff"][0] + F_HEADS],
         jnp.zeros(w_in.shape[:2] + (SMALL_W - M_HEADS - F_HEADS,), w_in.dtype)], axis=-1).astype(BF16)
    return segs


def kernel(x_prompt, x_sample, cache_a_k, cache_a_v, state_conv, state_ssm, cache_f_k, cache_f_v, cache_f_logf, norm_ffn1, w_ffn1_in, w_ffn1_out, norm_mix, w_in, b_gate, a_q_norm, a_k_norm, a_lambda, a_out_norm, m_conv_w, m_conv_b, m_dt_bias, m_a_log, m_d, m_norm, f_q_norm, f_k_norm, f_b, w_branch, w_out, norm_ffn2, w_ffn2_in, w_ffn2_out):
    depth = w_in.shape[0]
    bp, tp, d = x_prompt.shape
    bs, ts, _ = x_sample.shape
    past = cache_a_k.shape[2]

    wts = dict(ffn1_in=(w_ffn1_in * norm_ffn1[:, :, None]).astype(BF16), ffn1_out=w_ffn1_out.astype(BF16),
               w_in=_prep_w_in(w_in * norm_mix[:, :, None], d),
               w_branch=w_branch.astype(BF16), w_out=w_out.astype(BF16),
               ffn2_in=(w_ffn2_in * norm_ffn2[:, :, None]).astype(BF16), ffn2_out=w_ffn2_out.astype(BF16))

    def row3(p):
        return p.reshape(depth, 1, p.shape[-1])

    pr = dict(b_gate=b_gate.reshape(depth, 1, N_BRANCH * d),
              a_q_norm=row3(a_q_norm), a_k_norm=row3(a_k_norm), a_lambda=a_lambda, a_out_norm=row3(a_out_norm),
              conv_w=m_conv_w, conv_b=row3(m_conv_b), m_norm=row3(m_norm),
              dt_bias_c=_lanes(m_dt_bias, DT_LANE0), a_log_c=_lanes(m_a_log, DT_LANE0), d_c=_lanes(m_d, DT_LANE0),
              f_q_norm=row3(f_q_norm), f_k_norm=row3(f_k_norm), f_b=_lanes(f_b, FF_LANE0))

    tabs_p = _rope_tables(jnp.arange(tp, dtype=jnp.int32))
    tabs_s = _rope_tables(past + jnp.arange(ts, dtype=jnp.int32))

    st_p = (None, None, jnp.zeros((1, bp, M_CONV - 1, M_CONV_CH), F32),
            jnp.zeros((1, bp, M_HEADS * M_HEAD_DIM, M_STATE), F32), None, None, None)
    st_s = (cache_a_k.reshape(depth, bs, past, 2 * A_HEADS, A_HD), cache_a_v,
            state_conv, state_ssm.reshape(depth, bs, M_HEADS * M_HEAD_DIM, M_STATE),
            cache_f_k, cache_f_v, cache_f_logf)

    x_p, x_s = x_prompt.reshape(bp * tp, d), x_sample.reshape(bs * ts, d)
    xs_p, xs_s = (x_p, *_cast_sumsq(x_p)), (x_s, *_cast_sumsq(x_s))
    kv_p = kv_s = None
    new_p, new_s = [], []
    for layer in range(depth):
        last = layer == depth - 1
        xs_p, kv_p, n_p = _layer(xs_p, bp, tp, st_p, 0, wts, pr, tabs_p, layer, depth, 0, 0, kv_p, last)
        xs_s, kv_s, n_s = _layer(xs_s, bs, ts, st_s, past, wts, pr, tabs_s, layer, depth, layer, layer, kv_s, last)
        new_p.append(n_p)
        new_s.append(n_s)
    y_p, y_s = xs_p[0].reshape(bp, tp, d), xs_s[0].reshape(bs, ts, d)

    def stack(rows, i):
        return jnp.stack([r[i] for r in rows], axis=0)

    def group_out(y, kv, new, b, t):
        ak_n, av_n, fk_n, fv_n = kv
        return (ak_n.reshape(depth, b, t, A_HEADS, 2, A_HD), av_n, stack(new, 0), stack(new, 1),
                fk_n, fv_n, stack(new, 2))

    out_p = group_out(y_p, kv_p, new_p, bp, tp)
    out_s = group_out(y_s, kv_s, new_s, bs, ts)
    return (y_p, y_s) + out_p + out_s
```
